```python
import math
import jax, jax.numpy as jnp
from jax import lax
import numpy as np

D_MODEL = 1024
BATCH = 2
SEQ = 8192
DEPTH = 2

N_BRANCH = 4
BRANCH_WIDTH = D_MODEL // 2
POOL_WINDOWS = (2, 4, 8, 16)
POOL_GROUPS = len(POOL_WINDOWS)
POOL_GROUP_DIM = BRANCH_WIDTH // POOL_GROUPS
HGRN_HEAD_DIM = 128
HGRN_HEADS = BRANCH_WIDTH // HGRN_HEAD_DIM
HGRN_CHUNK = 64
GMLP_CHUNK = 128
GMLP_GROUPS = 4
GMLP_GROUP_DIM = BRANCH_WIDTH // GMLP_GROUPS
FOX_HEAD_DIM = 128
FOX_HEADS = BRANCH_WIDTH // FOX_HEAD_DIM
FOX_BLOCK = 128
N_IN = BRANCH_WIDTH * (1 + 4 + 2 + 3) + FOX_HEADS
PEER_HEADS = 8
PEER_N_KEYS = 128
PEER_N_EXPERTS = PEER_N_KEYS * PEER_N_KEYS
PEER_TOPK = 16
PEER_QUERY_DIM = 256
PEER_HALF = PEER_QUERY_DIM // 2
PEER_BLOCK = 128
ALPHA = (2.0 * DEPTH) ** 0.25
BETA = (8.0 * DEPTH) ** -0.25
LN_EPS = 1e-5
RMS_EPS = 1e-6
NEG_BIG = -1e30

kernel_name = "hybrid_pool_hgrn2_gmlp_fox_peer_deepnorm"


def layer_norm(x, g, b):
    xf = x.astype(jnp.float32)
    mu = jnp.mean(xf, -1, keepdims=True)
    var = jnp.mean(jnp.square(xf - mu), -1, keepdims=True)
    return ((xf - mu) * lax.rsqrt(var + LN_EPS) * g + b).astype(x.dtype)


def rms_norm(x, g):
    xf = x.astype(jnp.float32)
    return xf * lax.rsqrt(jnp.mean(jnp.square(xf), -1, keepdims=True) + RMS_EPS) * g


def pool_mixer(p, pool_w, pool_scale):
    B, S, _ = p.shape
    pf = p.astype(jnp.float32)
    cs = jnp.cumsum(pf, axis=1)
    cs = jnp.concatenate([jnp.zeros_like(cs[:, :1]), cs], axis=1)
    t1 = jnp.arange(1, S + 1, dtype=jnp.float32)[None, :, None]
    outs = []
    for g, w in enumerate(POOL_WINDOWS):
        sl = slice(g * POOL_GROUP_DIM, (g + 1) * POOL_GROUP_DIM)
        cg = cs[..., sl]
        lo = jnp.pad(cg[:, :S + 1 - w], ((0, 0), (w - 1, 0), (0, 0)))
        mean = (cg[:, 1:] - lo) / jnp.minimum(t1, float(w))
        outs.append(mean - pf[..., sl])
    d = jnp.stack(outs, axis=2).astype(p.dtype)
    y = jnp.einsum('bsgc,gcd->bsgd', d, pool_w).reshape(B, S, BRANCH_WIDTH)
    return y * pool_scale


def hgrn2_mixer(q, f_raw, i, g, lb, norm_g):
    B, S, _ = q.shape
    H, Dh, C = HGRN_HEADS, HGRN_HEAD_DIM, HGRN_CHUNK
    n = S // C
    zf = f_raw.astype(jnp.float32)
    k = (1.0 - lb) * jax.nn.sigmoid(-zf)
    log_f = jnp.log1p(-k)

    def chunks(z):
        return z.astype(jnp.float32).reshape(B, n, C, H, Dh).transpose(1, 0, 3, 2, 4)

    mask = jnp.tril(jnp.ones((C, C), bool))[None, None, :, :, None]

    def step(state, inp):
        qc, kc, vc, lc = inp
        b = jnp.cumsum(lc, axis=2)
        o_inter = jnp.einsum('bhtk,bhkv->bhtv', qc * jnp.exp(b), state)
        diff = b[:, :, :, None, :] - b[:, :, None, :, :]
        decay = jnp.exp(jnp.where(mask, diff, NEG_BIG))
        a = jnp.einsum('bhtk,bhtsk,bhsk->bhts', qc, decay, kc)
        o = o_inter + jnp.einsum('bhts,bhsv->bhtv', a, vc)
        b_last = b[:, :, -1:, :]
        new_state = jnp.exp(b_last[:, :, 0, :, None]) * state + jnp.einsum(
            'bhsk,bhsv->bhkv', kc * jnp.exp(b_last - b), vc)
        return new_state, o

    s0 = jnp.zeros((B, H, Dh, Dh), jnp.float32)
    _, o = lax.scan(step, s0, (chunks(jax.nn.silu(q.astype(jnp.float32))), chunks(k), chunks(i), chunks(log_f)))
    o = o.transpose(1, 0, 3, 2, 4).reshape(B, S, H, Dh)
    o = rms_norm(o, norm_g).reshape(B, S, BRANCH_WIDTH)
    return (o * jax.nn.silu(g.astype(jnp.float32))).astype(q.dtype)


def gmlp_mixer(u, v, ln_g, ln_b, ws, bs):
    B, S, _ = u.shape
    n = S // GMLP_CHUNK
    u = jax.nn.gelu(u)
    v = layer_norm(jax.nn.gelu(v), ln_g, ln_b)
    vc = v.reshape(B, n, GMLP_CHUNK, GMLP_GROUPS, GMLP_GROUP_DIM)
    ws_causal = ws * jnp.tril(jnp.ones((GMLP_CHUNK, GMLP_CHUNK), ws.dtype))
    s = jnp.einsum('gts,bnsgc->bntgc', ws_causal, vc) + bs.T[None, None, :, :, None]
    return u * s.reshape(B, S, BRANCH_WIDTH)


def fox_mixer(q, k, v, f_logit):
    B, S, _ = q.shape
    H, Dh, Bq = FOX_HEADS, FOX_HEAD_DIM, FOX_BLOCK
    nb = S // Bq
    c = jnp.cumsum(jax.nn.log_sigmoid(f_logit.astype(jnp.float32)), axis=1).transpose(0, 2, 1)
    qh = q.reshape(B, S, H, Dh).transpose(0, 2, 1, 3) * (Dh ** -0.5)
    kh = k.reshape(B, S, H, Dh).transpose(0, 2, 1, 3)
    vh = v.reshape(B, S, H, Dh).transpose(0, 2, 1, 3)
    qb = qh.reshape(B, H, nb, Bq, Dh).transpose(2, 0, 1, 3, 4)
    cb = c.reshape(B, H, nb, Bq).transpose(2, 0, 1, 3)
    pos_k = jnp.arange(S)

    def block(args):
        qi, ci, blk = args
        pos_q = blk * Bq + jnp.arange(Bq)
        logits = jnp.einsum('bhtd,bhsd->bhts', qi.astype(jnp.float32), kh.astype(jnp.float32))
        logits = logits + ci[..., None] - c[:, :, None, :]
        logits = jnp.where(pos_q[:, None] >= pos_k[None, :], logits, NEG_BIG)
        p = jax.nn.softmax(logits, axis=-1)
        return jnp.einsum('bhts,bhsd->bhtd', p.astype(vh.dtype), vh)

    o = lax.map(block, (qb, cb, jnp.arange(nb)))
    return o.transpose(1, 0, 3, 2, 4).reshape(B, S, BRANCH_WIDTH)


def token_mixer_block(x, w_in, b_fox_f, pool_w, pool_scale, lb, hgrn_norm_g,
                      gmlp_ln_g, gmlp_ln_b, gmlp_ws, gmlp_bs, w_gate, b_gate, w_branch, w_out):
    B, S, D = x.shape
    W = BRANCH_WIDTH
    h = jnp.einsum('bsd,dn->bsn', x, w_in)
    splits = [W * j for j in range(1, 11)]
    (p_in, hq, hf, hi, hg, gu, gv, fq, fk, fv, ff) = jnp.split(h, splits, axis=-1)
    a = pool_mixer(p_in, pool_w, pool_scale)
    b = hgrn2_mixer(hq, hf, hi, hg, lb, hgrn_norm_g)
    c = gmlp_mixer(gu, gv, gmlp_ln_g, gmlp_ln_b, gmlp_ws, gmlp_bs)
    d = fox_mixer(fq, fk, fv, ff + b_fox_f)
    br = jnp.stack([a, b, c, d], axis=2)
    gates = jax.nn.sigmoid(jnp.einsum('bsd,dn->bsn', x, w_gate) + b_gate).reshape(B, S, N_BRANCH, D)
    merged = jnp.einsum('bsmw,mwd,bsmd->bsd', br, w_branch, gates)
    return jnp.einsum('bsd,de->bse', merged, w_out)


def peer_ffn(x, wq, keys, u_tab, v_tab):
    B, S, D = x.shape
    T = B * S
    xt = x.reshape(T, D)
    q = jnp.einsum('td,dn->tn', xt, wq).reshape(T, PEER_HEADS, 2, PEER_HALF).astype(jnp.float32)
    s1 = jnp.einsum('thd,hkd->thk', q[:, :, 0], keys[0].astype(jnp.float32))
    s2 = jnp.einsum('thd,hkd->thk', q[:, :, 1], keys[1].astype(jnp.float32))
    v1, i1 = lax.top_k(s1, PEER_TOPK)
    v2, i2 = lax.top_k(s2, PEER_TOPK)
    cand = (v1[..., :, None] + v2[..., None, :]).reshape(T, PEER_HEADS, PEER_TOPK * PEER_TOPK)
    cidx = (i1[..., :, None] * PEER_N_KEYS + i2[..., None, :]).reshape(T, PEER_HEADS, PEER_TOPK * PEER_TOPK)
    top_s, pos = lax.top_k(cand, PEER_TOPK)
    eidx = jnp.take_along_axis(cidx, pos, axis=-1)
    gate = jax.nn.softmax(top_s, axis=-1).astype(x.dtype)
    nblk = T // PEER_BLOCK

    def blk(args):
        xb, eb, gb = args
        hidden = jax.nn.gelu(jnp.einsum('td,thkd->thk', xb, u_tab[eb]))
        return jnp.einsum('thk,thkd->td', gb * hidden, v_tab[eb])

    y = lax.map(blk, (xt.reshape(nblk, PEER_BLOCK, D),
                      eidx.reshape(nblk, PEER_BLOCK, PEER_HEADS, PEER_TOPK),
                      gate.reshape(nblk, PEER_BLOCK, PEER_HEADS, PEER_TOPK)))
    return y.reshape(B, S, D)


def setup_inputs(seed: int = 0) -> dict:
    key = jax.random.key(seed)
    ks = jax.random.split(key, 24)
    f32 = jnp.float32
    L, D, W = DEPTH, D_MODEL, BRANCH_WIDTH

    def nrm(k, shape, scale):
        return jax.random.normal(k, shape, f32) * scale

    return {
        "x": nrm(ks[0], (BATCH, SEQ, D), 1.0),
        "w_in": nrm(ks[1], (L, D, N_IN), D ** -0.5),
        "b_fox_f": nrm(ks[2], (L, FOX_HEADS), 0.1),
        "pool_w": nrm(ks[3], (L, POOL_GROUPS, POOL_GROUP_DIM, POOL_GROUP_DIM), POOL_GROUP_DIM ** -0.5),
        "pool_scale": 1.0 + nrm(ks[4], (L, W), 0.02),
        "hgrn_lb_logits": nrm(ks[5], (L, W), 0.1),
        "hgrn_norm_g": 1.0 + nrm(ks[6], (L, HGRN_HEAD_DIM), 0.02),
        "gmlp_ln_g": 1.0 + nrm(ks[7], (L, W), 0.02),
        "gmlp_ln_b": nrm(ks[8], (L, W), 0.02),
        "gmlp_ws": nrm(ks[9], (L, GMLP_GROUPS, GMLP_CHUNK, GMLP_CHUNK), 0.02),
        "gmlp_bs": 1.0 + nrm(ks[10], (L, GMLP_GROUPS, GMLP_CHUNK), 0.01),
        "w_gate": nrm(ks[11], (L, D, N_BRANCH * D), D ** -0.5),
        "b_gate": nrm(ks[12], (L, N_BRANCH * D), 0.02),
        "w_branch": nrm(ks[13], (L, N_BRANCH, W, D), BETA * W ** -0.5),
        "w_out": nrm(ks[14], (L, D, D), BETA * D ** -0.5),
        "ln1_g": 1.0 + nrm(ks[15], (L, D), 0.02),
        "ln1_b": nrm(ks[16], (L, D), 0.02),
        "peer_wq": nrm(ks[17], (L, D, PEER_HEADS * PEER_QUERY_DIM), D ** -0.5),
        "peer_keys": nrm(ks[18], (L, 2, PEER_HEADS, PEER_N_KEYS, PEER_HALF), PEER_HALF ** -0.5),
        "peer_u": nrm(ks[19], (L, PEER_N_EXPERTS, D), D ** -0.5),
        "peer_v": nrm(ks[20], (L, PEER_N_EXPERTS, D), BETA),
        "ln2_g": 1.0 + nrm(ks[21], (L, D), 0.02),
        "ln2_b": nrm(ks[22], (L, D), 0.02),
    }


def reference(x, w_in, b_fox_f, pool_w, pool_scale, hgrn_lb_logits, hgrn_norm_g, gmlp_ln_g, gmlp_ln_b,
              gmlp_ws, gmlp_bs, w_gate, b_gate, w_branch, w_out, ln1_g, ln1_b, peer_wq, peer_keys,
              peer_u, peer_v, ln2_g, ln2_b):
    p_lb = jax.nn.softmax(hgrn_lb_logits.astype(jnp.float32), axis=0)
    lower_bounds = jnp.cumsum(p_lb, axis=0) - p_lb[0]
    for l in range(DEPTH):
        mix = token_mixer_block(x, w_in[l], b_fox_f[l], pool_w[l], pool_scale[l], lower_bounds[l],
                                hgrn_norm_g[l], gmlp_ln_g[l], gmlp_ln_b[l], gmlp_ws[l], gmlp_bs[l],
                                w_gate[l], b_gate[l], w_branch[l], w_out[l])
        x = layer_norm(ALPHA * x + mix, ln1_g[l], ln1_b[l])
        ffn = peer_ffn(x, peer_wq[l], peer_keys[l], peer_u[l], peer_v[l])
        x = layer_norm(ALPHA * x + ffn, ln2_g[l], ln2_b[l])
    return x
```

```python
import functools

import numpy as np
import jax
import jax.numpy as jnp
from jax import lax
from jax.experimental import pallas as pl
from jax.experimental.pallas import tpu as pltpu

F32 = jnp.float32
BF16 = jnp.bfloat16

D_MODEL = 1024
BRANCH_WIDTH = 512
HEAD_DIM = 128
N_HEADS = BRANCH_WIDTH // HEAD_DIM
POOL_WINDOWS = (2, 4, 8, 16)
POOL_HALO = 16
GMLP_CHUNK = 128
HGRN_CHUNK = 128
HGRN_SUB = 16
PEER_HEADS = 8
PEER_N_KEYS = 128
PEER_TOPK = 16
LN_EPS = 1e-5
RMS_EPS = 1e-6
NEG_BIG = -1e30
TINY = 1e-37

LANES = 128
VMEM_LIMIT = 48 * 1024 * 1024

NT_DIMS = (((1,), (1,)), ((), ()))


def _params(n_grid, vmem=VMEM_LIMIT):
    return pltpu.CompilerParams(dimension_semantics=("arbitrary",) * n_grid, vmem_limit_bytes=vmem)


def _gelu(x):
    return 0.5 * x * (1.0 + jnp.tanh(0.7978845608028654 * (x + 0.044715 * (x * x * x))))


def _sigmoid(x):
    return 1.0 / (1.0 + jnp.exp(-x))


def _split3(x):
    h1 = x.astype(BF16)
    r1 = x - h1.astype(F32)
    h2 = r1.astype(BF16)
    h3 = (r1 - h2.astype(F32)).astype(BF16)
    return h1, h2, h3


def _dot3(m, x):
    h1, h2, h3 = _split3(x)
    return (jnp.dot(m, h1, preferred_element_type=F32) + jnp.dot(m, h2, preferred_element_type=F32)
            + jnp.dot(m, h3, preferred_element_type=F32))


def _layer_norm(y, g, b):
    mu = jnp.mean(y, axis=-1, keepdims=True)
    yc = y - mu
    var = jnp.mean(yc * yc, axis=-1, keepdims=True)
    return yc * lax.rsqrt(var + LN_EPS) * g + b


def _mm_kernel(x_ref, w_ref, o_ref):
    o_ref[...] = jnp.dot(x_ref[...], w_ref[...], preferred_element_type=F32).astype(o_ref.dtype)


def _matmul(x, w, out_dtype, tm, tn):
    m, k = x.shape
    n = w.shape[1]
    return pl.pallas_call(
        _mm_kernel,
        grid=(m // tm, n // tn),
        in_specs=[pl.BlockSpec((tm, k), lambda i, j: (i, 0)), pl.BlockSpec((k, tn), lambda i, j: (0, j))],
        out_specs=pl.BlockSpec((tm, tn), lambda i, j: (i, j)),
        out_shape=jax.ShapeDtypeStruct((m, n), out_dtype),
        compiler_params=_params(2),
        name="proj_matmul",
    )(x, w)


def _local_kernel(p_ref, halo_ref, gu_ref, gv_ref, sm_ref, sh_ref, pw_ref, ps_ref, lg_ref, lb_ref, ws_ref,
                  bsb_ref, a_ref, c_ref, *, rows, tiles_per_seq):
    t_in_seq = pl.program_id(0) % tiles_per_seq
    p = p_ref[...]
    halo = halo_ref[...]
    halo = jnp.where(t_in_seq == 0, jnp.zeros_like(halo), halo)
    pos = t_in_seq * rows + lax.broadcasted_iota(jnp.int32, (rows, HEAD_DIM), 0)
    for g, w in enumerate(POOL_WINDOWS):
        sl = slice(g * HEAD_DIM, (g + 1) * HEAD_DIM)
        sums = (jnp.dot(sm_ref[g], p[:, sl], preferred_element_type=F32)
                + jnp.dot(sh_ref[g], halo[:, sl], preferred_element_type=F32))
        cnt = jnp.minimum(pos + 1, w).astype(F32)
        d = sums / cnt - p[:, sl].astype(F32)
        y = jnp.dot(d.astype(BF16), pw_ref[g], preferred_element_type=F32) * ps_ref[:, sl]
        a_ref[:, sl] = y.astype(a_ref.dtype)

    u = _gelu(gu_ref[...].astype(F32))
    v = _layer_norm(_gelu(gv_ref[...].astype(F32)), lg_ref[...], lb_ref[...]).astype(BF16)
    r = lax.broadcasted_iota(jnp.int32, (GMLP_CHUNK, GMLP_CHUNK), 0)
    c = lax.broadcasted_iota(jnp.int32, (GMLP_CHUNK, GMLP_CHUNK), 1)
    for g in range(N_HEADS):
        sl = slice(g * HEAD_DIM, (g + 1) * HEAD_DIM)
        wsc = jnp.where(r >= c, ws_ref[g], 0.0).astype(BF16)
        for ch in range(rows // GMLP_CHUNK):
            rs = slice(ch * GMLP_CHUNK, (ch + 1) * GMLP_CHUNK)
            s = jnp.dot(wsc, v[rs, sl], preferred_element_type=F32) + bsb_ref[:, sl]
            c_ref[rs, sl] = (u[rs, sl] * s).astype(c_ref.dtype)


def _pool_sum_matrices(rows):
    t = np.arange(rows)[:, None]
    s = np.arange(rows)[None, :]
    u = np.arange(POOL_HALO)[None, :]
    main = np.stack([((s <= t) & (s > t - w)) for w in POOL_WINDOWS]).astype(np.float32)
    halo = np.stack([(u >= t - w + POOL_HALO + 1) for w in POOL_WINDOWS]).astype(np.float32)
    return jnp.asarray(main, BF16), jnp.asarray(halo, BF16)


def _local_mixers(h, seq, pool_w, pool_scale, ln_g, ln_b, ws, bs, rows=256):
    t = h.shape[0]
    tiles_per_seq = seq // rows
    sm, sh = _pool_sum_matrices(rows)
    bsb = jnp.repeat(bs.T, HEAD_DIM, axis=1)
    halo_blocks = rows // POOL_HALO
    const = lambda *shape: pl.BlockSpec(shape, lambda i: (0,) * len(shape))
    return pl.pallas_call(
        functools.partial(_local_kernel, rows=rows, tiles_per_seq=tiles_per_seq),
        grid=(t // rows,),
        in_specs=[
            pl.BlockSpec((rows, BRANCH_WIDTH), lambda i: (i, 0)),
            pl.BlockSpec((POOL_HALO, BRANCH_WIDTH), lambda i: (jnp.maximum(i * halo_blocks - 1, 0), 0)),
            pl.BlockSpec((rows, BRANCH_WIDTH), lambda i: (i, 5)),
            pl.BlockSpec((rows, BRANCH_WIDTH), lambda i: (i, 6)),
            const(4, rows, rows), const(4, rows, POOL_HALO), const(4, HEAD_DIM, HEAD_DIM),
            const(1, BRANCH_WIDTH), const(1, BRANCH_WIDTH), const(1, BRANCH_WIDTH),
            const(4, GMLP_CHUNK, GMLP_CHUNK), const(GMLP_CHUNK, BRANCH_WIDTH),
        ],
        out_specs=[pl.BlockSpec((rows, BRANCH_WIDTH), lambda i: (i, 0))] * 2,
        out_shape=[jax.ShapeDtypeStruct((t, BRANCH_WIDTH), BF16)] * 2,
        compiler_params=_params(1),
        name="pool_gmlp",
    )(h, h, h, h, sm, sh, pool_w.astype(BF16), pool_scale[None, :], ln_g[None, :], ln_b[None, :], ws, bsb)


def _hgrn_kernel(q_ref, f_ref, i_ref, g_ref, lb_ref, ng_ref, tri_ref, o_ref, st_ref):
    C, S = HGRN_CHUNK, HGRN_SUB

    @pl.when(pl.program_id(1) == 0)
    def _():
        st_ref[...] = jnp.zeros_like(st_ref)

    zq = q_ref[...].astype(F32)
    qs = zq * _sigmoid(zq)
    kk = (1.0 - lb_ref[...]) * _sigmoid(-f_ref[...].astype(F32))
    logf = jnp.log(1.0 - kk)
    vv = i_ref[...].astype(F32)
    vb = vv.astype(BF16)
    bcs = _dot3(tri_ref[...], logf)
    b_last = bcs[C - 1:C, :]
    qe = (qs * jnp.exp(bcs)).astype(BF16)
    kd = (kk * jnp.exp(b_last - bcs)).astype(BF16)
    decay = jnp.exp(b_last)

    row = lax.broadcasted_iota(jnp.int32, (C, HEAD_DIM), 0)
    row_in_sub = row % S

    o = jnp.zeros((C, BRANCH_WIDTH), F32)
    d_acc = jnp.zeros((C, BRANCH_WIDTH), F32)
    for delta in range(S):
        if delta == 0:
            x = qs * kk
            v_s = vv
        else:
            d_acc = d_acc + pltpu.roll(logf, delta - 1, 0) if delta > 1 else d_acc + logf
            x = qs * pltpu.roll(kk, delta, 0) * jnp.exp(d_acc)
            v_s = pltpu.roll(vv, delta, 0)
        parts = []
        for h in range(N_HEADS):
            sl = slice(h * HEAD_DIM, (h + 1) * HEAD_DIM)
            w = jnp.sum(x[:, sl], axis=-1, keepdims=True)
            parts.append(jnp.where(row_in_sub >= delta, w, 0.0) * v_s[:, sl])
        o = o + jnp.concatenate(parts, axis=1)

    outs = []
    for h in range(N_HEADS):
        sl = slice(h * HEAD_DIM, (h + 1) * HEAD_DIM)
        st = st_ref[h]
        o_h = o[:, sl] + lax.dot_general(qe[:, sl], st.astype(BF16), NT_DIMS, preferred_element_type=F32)
        blocks = [jnp.zeros((S, HEAD_DIM), F32)]
        for sb in range(1, C // S):
            r0 = sb * S
            ref = bcs[r0 - 1:r0, sl]
            q_i = (qs[r0:r0 + S, sl] * jnp.exp(bcs[r0:r0 + S, sl] - ref)).astype(BF16)
            k_i = (kk[:, sl] * jnp.exp(jnp.where(row < r0, ref - bcs[:, sl], NEG_BIG))).astype(BF16)
            a_i = lax.dot_general(q_i, k_i, NT_DIMS, preferred_element_type=F32)
            blocks.append(jnp.dot(a_i.astype(BF16), vb[:, sl], preferred_element_type=F32))
        o_h = o_h + jnp.concatenate(blocks, axis=0)
        st_ref[h] = st * decay[:, sl] + jnp.dot(vv[:, sl].T.astype(BF16), kd[:, sl], preferred_element_type=F32)
        ms = jnp.mean(o_h * o_h, axis=-1, keepdims=True)
        outs.append(o_h * lax.rsqrt(ms + RMS_EPS))
    zg = g_ref[...].astype(F32)
    o_ref[...] = (jnp.concatenate(outs, axis=1) * ng_ref[...] * (zg * _sigmoid(zg))).astype(o_ref.dtype)


def _hgrn(h, batch, seq, lower_bound, norm_g):
    t = h.shape[0]
    C = HGRN_CHUNK
    n = seq // C
    tri = jnp.asarray(np.tril(np.ones((C, C), np.float32)), BF16)
    blk = lambda col: pl.BlockSpec((C, BRANCH_WIDTH), lambda b, c: (b * n + c, col))
    const = lambda *shape: pl.BlockSpec(shape, lambda b, c: (0,) * len(shape))
    return pl.pallas_call(
        _hgrn_kernel,
        grid=(batch, n),
        in_specs=[blk(1), blk(2), blk(3), blk(4), const(1, BRANCH_WIDTH), const(1, BRANCH_WIDTH), const(C, C)],
        out_specs=pl.BlockSpec((C, BRANCH_WIDTH), lambda b, c: (b * n + c, 0)),
        out_shape=jax.ShapeDtypeStruct((t, BRANCH_WIDTH), BF16),
        scratch_shapes=[pltpu.VMEM((N_HEADS, HEAD_DIM, HEAD_DIM), F32)],
        compiler_params=_params(2),
        name="hgrn2",
    )(h, h, h, h, lower_bound[None, :], jnp.tile(norm_g, N_HEADS)[None, :], tri)


def _fox_decay_kernel(f_ref, bias_ref, up_ref, low_ref, c_ref):
    z = f_ref[...] + bias_ref[...]
    ls = jnp.minimum(z, 0.0) - jnp.log(1.0 + jnp.exp(-jnp.abs(z)))
    within = _dot3_right(ls, up_ref[...])
    tot = jnp.broadcast_to(within[:, LANES - 1:LANES], within.shape)
    c_ref[...] = within + _dot3(low_ref[...], tot)


def _dot3_right(x, m):
    h1, h2, h3 = _split3(x)
    return (jnp.dot(h1, m, preferred_element_type=F32) + jnp.dot(h2, m, preferred_element_type=F32)
            + jnp.dot(h3, m, preferred_element_type=F32))


def _fox_decay(ffp, b_fox_f, batch, seq):
    nrow = seq // LANES
    f = ffp[:, :N_HEADS].reshape(batch, seq, N_HEADS).transpose(0, 2, 1).reshape(batch * N_HEADS * nrow, LANES)
    rows = f.shape[0]
    bias = jnp.broadcast_to(jnp.tile(jnp.repeat(b_fox_f, nrow), batch)[:, None], (rows, LANES))
    up = jnp.asarray(np.triu(np.ones((LANES, LANES), np.float32)), BF16)
    grp = np.arange(rows) // nrow
    low = (grp[:, None] == grp[None, :]) & (np.arange(rows)[None, :] < np.arange(rows)[:, None])
    low = jnp.asarray(low.astype(np.float32), BF16)
    full = lambda *shape: pl.BlockSpec(shape, lambda: (0,) * len(shape))
    c = pl.pallas_call(
        _fox_decay_kernel,
        in_specs=[full(rows, LANES), full(rows, LANES), full(LANES, LANES), full(rows, rows)],
        out_specs=full(rows, LANES),
        out_shape=jax.ShapeDtypeStruct((rows, LANES), F32),
        name="fox_decay",
    )(f, bias, up, low)
    return c.reshape(batch * N_HEADS, seq)


def _fox_kernel(q_ref, k_ref, v_ref, cq_ref, ck_ref, o_ref, m_ref, l_ref, acc_ref, *, tq, tk):
    qi, ki = pl.program_id(1), pl.program_id(2)

    @pl.when(ki == 0)
    def _():
        m_ref[...] = jnp.full_like(m_ref, NEG_BIG)
        l_ref[...] = jnp.zeros_like(l_ref)
        acc_ref[...] = jnp.zeros_like(acc_ref)

    @pl.when(ki <= qi)
    def _():
        q = (q_ref[...].astype(F32) * (HEAD_DIM ** -0.5)).astype(BF16)
        s = lax.dot_general(q, k_ref[...], NT_DIMS, preferred_element_type=F32)
        s = s + (cq_ref[0, :, 0:1] - ck_ref[0])
        pos_q = qi * tq + lax.broadcasted_iota(jnp.int32, (tq, tk), 0)
        pos_k = ki * tk + lax.broadcasted_iota(jnp.int32, (tq, tk), 1)
        s = jnp.where(pos_q >= pos_k, s, NEG_BIG)
        m_prev = m_ref[...]
        m_new = jnp.maximum(m_prev, jnp.max(s, axis=-1, keepdims=True))
        alpha = jnp.exp(m_prev - m_new)
        p = jnp.exp(s - m_new)
        l_ref[...] = alpha * l_ref[...] + jnp.sum(p, axis=-1, keepdims=True)
        acc_ref[...] = alpha * acc_ref[...] + jnp.dot(p.astype(BF16), v_ref[...], preferred_element_type=F32)
        m_ref[...] = m_new

    @pl.when(ki == pl.num_programs(2) - 1)
    def _():
        o_ref[...] = (acc_ref[...] / l_ref[...]).astype(o_ref.dtype)


def _fox(h, c, batch, seq, tq=512, tk=512):
    t = h.shape[0]
    nq, nk = seq // tq, seq // tk
    c3 = c[:, None, :]
    col0 = 7 * BRANCH_WIDTH // HEAD_DIM

    def qspec(off):
        return pl.BlockSpec((tq, HEAD_DIM), lambda bh, qi, ki: ((bh // N_HEADS) * nq + qi, col0 + off + bh % N_HEADS))

    def kspec(off):
        return pl.BlockSpec((tk, HEAD_DIM),
                            lambda bh, qi, ki: ((bh // N_HEADS) * nk + jnp.minimum(ki, qi), col0 + off + bh % N_HEADS))

    return pl.pallas_call(
        functools.partial(_fox_kernel, tq=tq, tk=tk),
        grid=(batch * N_HEADS, nq, nk),
        in_specs=[
            qspec(0), kspec(N_HEADS), kspec(2 * N_HEADS),
            pl.BlockSpec((1, 1, tq), lambda bh, qi, ki: (bh, 0, qi)),
            pl.BlockSpec((1, 1, tk), lambda bh, qi, ki: (bh, 0, jnp.minimum(ki, qi))),
        ],
        out_specs=pl.BlockSpec((tq, HEAD_DIM), lambda bh, qi, ki: ((bh // N_HEADS) * nq + qi, bh % N_HEADS)),
        out_shape=jax.ShapeDtypeStruct((t, BRANCH_WIDTH), BF16),
        scratch_shapes=[pltpu.VMEM((tq, 1), F32), pltpu.VMEM((tq, 1), F32), pltpu.VMEM((tq, HEAD_DIM), F32)],
        compiler_params=_params(3),
        name="fox_attention",
    )(h, h, h, c3, c3)


def _merge_kernel(x_ref, a_ref, b_ref, c_ref, d_ref, wg_ref, bg_ref, wb_ref, wo_ref, g_ref, beta_ref,
                  o_ref, ob_ref, *, alpha):
    x = x_ref[...]
    xb = x.astype(BF16)
    merged = jnp.zeros(x.shape, F32)
    for m, br in enumerate((a_ref, b_ref, c_ref, d_ref)):
        sl = slice(m * D_MODEL, (m + 1) * D_MODEL)
        gate = _sigmoid(jnp.dot(xb, wg_ref[:, sl], preferred_element_type=F32) + bg_ref[:, sl])
        merged = merged + jnp.dot(br[...], wb_ref[m], preferred_element_type=F32) * gate
    mix = jnp.dot(merged.astype(BF16), wo_ref[...], preferred_element_type=F32)
    y = _layer_norm(alpha * x + mix, g_ref[...], beta_ref[...])
    o_ref[...] = y
    ob_ref[...] = y.astype(BF16)


def _merge(x, branches, w_gate, b_gate, w_branch, w_out, ln_g, ln_b, alpha, tm=256):
    t = x.shape[0]
    row = lambda width: pl.BlockSpec((tm, width), lambda i: (i, 0))
    const = lambda *shape: pl.BlockSpec(shape, lambda i: (0,) * len(shape))
    return pl.pallas_call(
        functools.partial(_merge_kernel, alpha=alpha),
        grid=(t // tm,),
        in_specs=[row(D_MODEL)] + [row(BRANCH_WIDTH)] * 4 + [
            const(D_MODEL, 4 * D_MODEL), const(1, 4 * D_MODEL), const(4, BRANCH_WIDTH, D_MODEL),
            const(D_MODEL, D_MODEL), const(1, D_MODEL), const(1, D_MODEL)],
        out_specs=[row(D_MODEL), row(D_MODEL)],
        out_shape=[jax.ShapeDtypeStruct((t, D_MODEL), F32), jax.ShapeDtypeStruct((t, D_MODEL), BF16)],
        compiler_params=_params(1),
        name="merge_out_ln",
    )(x, *branches, w_gate.astype(BF16), b_gate[None, :], w_branch.astype(BF16), w_out.astype(BF16),
      ln_g[None, :], ln_b[None, :])


def _top_values(x, count, floor):
    tops = []
    for _ in range(count):
        m = jnp.max(x, axis=0, keepdims=True)
        tops.append(m)
        x = jnp.where(x >= m, floor, x)
    return tops


def _select_kernel(x_ref, wq_ref, k_ref, e1_ref, e2_ref, pt_ref):
    q = jnp.dot(x_ref[...], wq_ref[...], preferred_element_type=F32).astype(BF16)
    for h in range(PEER_HEADS):
        dense, top_e = [], []
        for half in range(2):
            c0 = (2 * h + half) * PEER_N_KEYS
            s = lax.dot_general(k_ref[half, h], q[:, c0:c0 + PEER_N_KEYS], NT_DIMS,
                                preferred_element_type=F32)
            tops = _top_values(s, PEER_TOPK, -jnp.inf)
            dense.append(jnp.where(s >= tops[-1], jnp.exp(s - tops[0]), 0.0))
            top_e.append(jnp.concatenate([jnp.exp(v - tops[0]) for v in tops], axis=0))
        cand = jnp.concatenate([top_e[0][a:a + 1, :] * top_e[1] for a in range(PEER_TOPK)], axis=0)
        best = _top_values(cand, PEER_TOPK + 1, -1.0)
        z = best[0]
        for v in best[1:PEER_TOPK]:
            z = z + v
        inv_z = 1.0 / z
        e1_ref[h] = dense[0] * inv_z
        e2_ref[h] = dense[1]
        pt_ref[h:h + 1, :] = jnp.maximum(0.5 * (best[PEER_TOPK - 1] + best[PEER_TOPK]) * inv_z, TINY)


def _peer_select(xb, wq, keys, tm=256):
    t = xb.shape[0]
    nq = wq.shape[1]
    const = lambda *shape: pl.BlockSpec(shape, lambda i: (0,) * len(shape))
    fac = jax.ShapeDtypeStruct((PEER_HEADS, PEER_N_KEYS, t), F32)
    return pl.pallas_call(
        _select_kernel,
        grid=(t // tm,),
        in_specs=[pl.BlockSpec((tm, D_MODEL), lambda i: (i, 0)), const(D_MODEL, nq),
                  const(2, PEER_HEADS, PEER_N_KEYS, PEER_N_KEYS)],
        out_specs=[pl.BlockSpec((PEER_HEADS, PEER_N_KEYS, tm), lambda i: (0, 0, i))] * 2
        + [pl.BlockSpec((PEER_HEADS, tm), lambda i: (0, i))],
        out_shape=[fac, fac, jax.ShapeDtypeStruct((PEER_HEADS, t), F32)],
        compiler_params=_params(1),
        name="peer_select",
    )(xb, wq.astype(BF16), keys.astype(BF16))


def _peer_dense_kernel(xb_ref, u_ref, vt_ref, e1_ref, e2_ref, pt_ref, x_ref, g_ref, beta_ref, o_ref, ob_ref,
                       acc_ref, ht_ref, wt_ref, *, alpha, tb, te):
    j = pl.program_id(1)
    rows_per_step = te // PEER_N_KEYS

    @pl.when(j == 0)
    def _():
        acc_ref[...] = jnp.zeros_like(acc_ref)

    ht_ref[...] = lax.dot_general(u_ref[...], xb_ref[...], NT_DIMS, preferred_element_type=F32)

    def body(rl, carry):
        r = j * rows_per_step + rl
        e0 = pl.multiple_of(rl * PEER_N_KEYS, PEER_N_KEYS)
        for lt in range(tb // LANES):
            ls = slice(lt * LANES, (lt + 1) * LANES)
            gate = jnp.zeros((PEER_N_KEYS, LANES), F32)
            e1_rows = e1_ref[r, :, ls]
            for h in range(PEER_HEADS):
                p = e1_rows[h:h + 1, :] * e2_ref[h, :, ls]
                gate = gate + jnp.where(p >= pt_ref[h:h + 1, ls], p, 0.0)
            hid = ht_ref[pl.ds(e0, PEER_N_KEYS), ls]
            wt_ref[pl.ds(e0, PEER_N_KEYS), ls] = (_gelu(hid) * gate).astype(BF16)
        return carry

    lax.fori_loop(0, rows_per_step, body, 0)
    acc_ref[...] += jnp.dot(vt_ref[...], wt_ref[...], preferred_element_type=F32)

    @pl.when(j == pl.num_programs(1) - 1)
    def _():
        y = _layer_norm(alpha * x_ref[...] + acc_ref[...].T, g_ref[...], beta_ref[...])
        o_ref[...] = y
        ob_ref[...] = y.astype(BF16)


def _peer_dense(x, xb, u_tab, vt_tab, e1, e2, pt, ln_g, ln_b, alpha, tb=512, te=512):
    t = x.shape[0]
    n_exp = u_tab.shape[0]
    const = lambda *shape: pl.BlockSpec(shape, lambda i, j: (0,) * len(shape))
    fac = pl.BlockSpec((PEER_HEADS, PEER_N_KEYS, tb), lambda i, j: (0, 0, i))
    row = pl.BlockSpec((tb, D_MODEL), lambda i, j: (i, 0))
    return pl.pallas_call(
        functools.partial(_peer_dense_kernel, alpha=alpha, tb=tb, te=te),
        grid=(t // tb, n_exp // te),
        in_specs=[row, pl.BlockSpec((te, D_MODEL), lambda i, j: (j, 0)),
                  pl.BlockSpec((D_MODEL, te), lambda i, j: (0, j)),
                  pl.BlockSpec((PEER_N_KEYS, PEER_HEADS, tb), lambda i, j: (0, 0, i)), fac,
                  pl.BlockSpec((PEER_HEADS, tb), lambda i, j: (0, i)), row, const(1, D_MODEL), const(1, D_MODEL)],
        out_specs=[row, row],
        out_shape=[jax.ShapeDtypeStruct((t, D_MODEL), F32), jax.ShapeDtypeStruct((t, D_MODEL), BF16)],
        scratch_shapes=[pltpu.VMEM((D_MODEL, tb), F32), pltpu.VMEM((te, tb), F32), pltpu.VMEM((te, tb), BF16)],
        compiler_params=_params(2),
        name="peer_dense",
    )(xb, u_tab, vt_tab, e1, e2, pt, x, ln_g[None, :], ln_b[None, :])


def kernel(x, w_in, b_fox_f, pool_w, pool_scale, hgrn_lb_logits, hgrn_norm_g, gmlp_ln_g, gmlp_ln_b, gmlp_ws, gmlp_bs,
           w_gate, b_gate, w_branch, w_out, ln1_g, ln1_b, peer_wq, peer_keys, peer_u, peer_v, ln2_g, ln2_b):
    batch, seq, d = x.shape
    depth = w_in.shape[0]
    alpha = (2.0 * depth) ** 0.25
    n_main = 10 * BRANCH_WIDTH

    p_lb = jax.nn.softmax(hgrn_lb_logits.astype(F32), axis=0)
    lower_bounds = jnp.cumsum(p_lb, axis=0) - p_lb[0]

    xf = x.reshape(batch * seq, d)
    xb = xf.astype(BF16)
    for l in range(depth):
        w_main = w_in[l][:, :n_main].astype(BF16)
        w_ff = jnp.pad(w_in[l][:, n_main:], ((0, 0), (0, LANES - N_HEADS))).astype(BF16)
        h = _matmul(xb, w_main, BF16, tm=1024, tn=1024)
        ffp = _matmul(xb, w_ff, F32, tm=2048, tn=LANES)

        a, c = _local_mixers(h, seq, pool_w[l], pool_scale[l], gmlp_ln_g[l], gmlp_ln_b[l], gmlp_ws[l], gmlp_bs[l])
        b = _hgrn(h, batch, seq, lower_bounds[l], hgrn_norm_g[l])
        cdec = _fox_decay(ffp, b_fox_f[l], batch, seq)
        dd = _fox(h, cdec, batch, seq)

        x1, x1b = _merge(xf, (a, b, c, dd), w_gate[l], b_gate[l], w_branch[l], w_out[l], ln1_g[l], ln1_b[l], alpha)

        e1, e2, pt = _peer_select(x1b, peer_wq[l], peer_keys[l])
        e1 = e1.transpose(1, 0, 2)
        xf, xb = _peer_dense(x1, x1b, peer_u[l].astype(BF16), peer_v[l].T.astype(BF16), e1, e2, pt,
                             ln2_g[l], ln2_b[l], alpha)
    return xf.reshape(batch, seq, d)
```

```python
import functools

import numpy as np
import jax
import jax.numpy as jnp
from jax import lax
from jax.experimental import pallas as pl
from jax.experimental.pallas import tpu as pltpu

F32 = jnp.float32
BF16 = jnp.bfloat16

D_MODEL = 1024
BRANCH_WIDTH = 512
HEAD_DIM = 128
N_HEADS = BRANCH_WIDTH // HEAD_DIM
POOL_WINDOWS = (2, 4, 8, 16)
POOL_HALO = 16
GMLP_CHUNK = 128
HGRN_CHUNK = 128
HGRN_SUB = 16
PEER_HEADS = 8
PEER_N_KEYS = 128
PEER_TOPK = 16
PEER_SUB = 32
PEER_PIECES = 2
LN_EPS = 1e-5
RMS_EPS = 1e-6
NEG_BIG = -1e30
FOX_UNDERFLOW = 110.0
FOX_BOUND_SLACK = 1.02

LANES = 128
VMEM_LIMIT = 48 * 1024 * 1024

NT_DIMS = (((1,), (1,)), ((), ()))
TN_DIMS = (((0,), (0,)), ((), ()))


def _params(n_grid, vmem=VMEM_LIMIT, flags=None):
    return pltpu.CompilerParams(dimension_semantics=("arbitrary",) * n_grid, vmem_limit_bytes=vmem, flags=flags)


def _gelu(x):
    return 0.5 * x * (1.0 + jnp.tanh(0.7978845608028654 * (x + 0.044715 * (x * x * x))))


def _gelu2(x):
    return x * (1.0 + jnp.tanh(x * (0.7978845608028654 + (0.7978845608028654 * 0.044715) * (x * x))))


def _sigmoid(x):
    return 1.0 / (1.0 + jnp.exp(-x))


def _split3(x):
    h1 = x.astype(BF16)
    r1 = x - h1.astype(F32)
    h2 = r1.astype(BF16)
    h3 = (r1 - h2.astype(F32)).astype(BF16)
    return h1, h2, h3


def _dot3(m, x):
    h1, h2, h3 = _split3(x)
    return (jnp.dot(m, h1, preferred_element_type=F32) + jnp.dot(m, h2, preferred_element_type=F32)
            + jnp.dot(m, h3, preferred_element_type=F32))


def _layer_norm(y, g, b):
    mu = jnp.mean(y, axis=-1, keepdims=True)
    yc = y - mu
    var = jnp.mean(yc * yc, axis=-1, keepdims=True)
    return yc * lax.rsqrt(var + LN_EPS) * g + b


def _mm_kernel(x_ref, w_ref, o_ref):
    o_ref[...] = jnp.dot(x_ref[...], w_ref[...].astype(BF16), preferred_element_type=F32).astype(o_ref.dtype)


def _matmul(x, w, out_dtype, tm, tn, n=None, layer=None):
    m, k = x.shape
    n = w.shape[-1] if n is None else n
    if layer is None:
        w_spec = pl.BlockSpec((k, tn), lambda i, j: (0, j))
    else:
        w_spec = pl.BlockSpec((None, k, tn), lambda i, j: (layer, 0, j))
    return pl.pallas_call(
        _mm_kernel,
        grid=(m // tm, n // tn),
        in_specs=[pl.BlockSpec((tm, k), lambda i, j: (i, 0)), w_spec],
        out_specs=pl.BlockSpec((tm, tn), lambda i, j: (i, j)),
        out_shape=jax.ShapeDtypeStruct((m, n), out_dtype),
        compiler_params=_params(2),
        name="proj_matmul",
    )(x, w)


def _local_kernel(p_ref, halo_ref, gu_ref, gv_ref, sm_ref, sh_ref, pw_ref, ps_ref, lg_ref, lb_ref, ws_ref,
                  bsb_ref, a_ref, c_ref, *, rows, tiles_per_seq):
    t_in_seq = pl.program_id(0) % tiles_per_seq
    p = p_ref[...]
    halo = halo_ref[...]
    halo = jnp.where(t_in_seq == 0, jnp.zeros_like(halo), halo)
    pos = t_in_seq * rows + lax.broadcasted_iota(jnp.int32, (rows, HEAD_DIM), 0)
    for g, w in enumerate(POOL_WINDOWS):
        sl = slice(g * HEAD_DIM, (g + 1) * HEAD_DIM)
        sums = (jnp.dot(sm_ref[g], p[:, sl], preferred_element_type=F32)
                + jnp.dot(sh_ref[g], halo[:, sl], preferred_element_type=F32))
        cnt = jnp.minimum(pos + 1, w).astype(F32)
        d = sums / cnt - p[:, sl].astype(F32)
        y = jnp.dot(d.astype(BF16), pw_ref[g], preferred_element_type=F32) * ps_ref[:, sl]
        a_ref[:, sl] = y.astype(a_ref.dtype)

    u = _gelu(gu_ref[...].astype(F32))
    v = _layer_norm(_gelu(gv_ref[...].astype(F32)), lg_ref[...], lb_ref[...]).astype(BF16)
    r = lax.broadcasted_iota(jnp.int32, (GMLP_CHUNK, GMLP_CHUNK), 0)
    c = lax.broadcasted_iota(jnp.int32, (GMLP_CHUNK, GMLP_CHUNK), 1)
    for g in range(N_HEADS):
        sl = slice(g * HEAD_DIM, (g + 1) * HEAD_DIM)
        wsc = jnp.where(r >= c, ws_ref[g], 0.0).astype(BF16)
        for ch in range(rows // GMLP_CHUNK):
            rs = slice(ch * GMLP_CHUNK, (ch + 1) * GMLP_CHUNK)
            s = jnp.dot(wsc, v[rs, sl], preferred_element_type=F32) + bsb_ref[:, sl]
            c_ref[rs, sl] = (u[rs, sl] * s).astype(c_ref.dtype)


def _pool_sum_matrices(rows):
    t = np.arange(rows)[:, None]
    s = np.arange(rows)[None, :]
    u = np.arange(POOL_HALO)[None, :]
    main = np.stack([((s <= t) & (s > t - w)) for w in POOL_WINDOWS]).astype(np.float32)
    halo = np.stack([(u >= t - w + POOL_HALO + 1) for w in POOL_WINDOWS]).astype(np.float32)
    return jnp.asarray(main, BF16), jnp.asarray(halo, BF16)


def _local_mixers(h, seq, pool_w, pool_scale, ln_g, ln_b, ws, bs, rows=256):
    t = h.shape[0]
    tiles_per_seq = seq // rows
    sm, sh = _pool_sum_matrices(rows)
    bsb = jnp.repeat(bs.T, HEAD_DIM, axis=1)
    halo_blocks = rows // POOL_HALO
    const = lambda *shape: pl.BlockSpec(shape, lambda i: (0,) * len(shape))
    return pl.pallas_call(
        functools.partial(_local_kernel, rows=rows, tiles_per_seq=tiles_per_seq),
        grid=(t // rows,),
        in_specs=[
            pl.BlockSpec((rows, BRANCH_WIDTH), lambda i: (i, 0)),
            pl.BlockSpec((POOL_HALO, BRANCH_WIDTH), lambda i: (jnp.maximum(i * halo_blocks - 1, 0), 0)),
            pl.BlockSpec((rows, BRANCH_WIDTH), lambda i: (i, 5)),
            pl.BlockSpec((rows, BRANCH_WIDTH), lambda i: (i, 6)),
            const(4, rows, rows), const(4, rows, POOL_HALO), const(4, HEAD_DIM, HEAD_DIM),
            const(1, BRANCH_WIDTH), const(1, BRANCH_WIDTH), const(1, BRANCH_WIDTH),
            const(4, GMLP_CHUNK, GMLP_CHUNK), const(GMLP_CHUNK, BRANCH_WIDTH),
        ],
        out_specs=[pl.BlockSpec((rows, BRANCH_WIDTH), lambda i: (i, 0))] * 2,
        out_shape=[jax.ShapeDtypeStruct((t, BRANCH_WIDTH), BF16)] * 2,
        compiler_params=_params(1),
        name="pool_gmlp",
    )(h, h, h, h, sm, sh, pool_w.astype(BF16), pool_scale[None, :], ln_g[None, :], ln_b[None, :], ws, bsb)


def _hgrn_kernel(q_ref, f_ref, i_ref, g_ref, lb_ref, ng_ref, tri_ref, o_ref, st_ref):
    C, S = HGRN_CHUNK, HGRN_SUB

    @pl.when(pl.program_id(1) == 0)
    def _():
        st_ref[...] = jnp.zeros_like(st_ref)

    zq = q_ref[...].astype(F32)
    qs = zq * _sigmoid(zq)
    kk = (1.0 - lb_ref[...]) * _sigmoid(-f_ref[...].astype(F32))
    logf = jnp.log(1.0 - kk)
    vv = i_ref[...].astype(F32)
    vb = vv.astype(BF16)
    bcs = _dot3(tri_ref[...], logf)
    b_last = bcs[C - 1:C, :]
    qe = (qs * jnp.exp(bcs)).astype(BF16)
    kd = (kk * jnp.exp(b_last - bcs)).astype(BF16)
    decay = jnp.exp(b_last)

    row = lax.broadcasted_iota(jnp.int32, (C, HEAD_DIM), 0)
    row_in_sub = row % S

    o = jnp.zeros((C, BRANCH_WIDTH), F32)
    d_acc = jnp.zeros((C, BRANCH_WIDTH), F32)
    for delta in range(S):
        if delta == 0:
            x = qs * kk
            v_s = vv
        else:
            d_acc = d_acc + pltpu.roll(logf, delta - 1, 0) if delta > 1 else d_acc + logf
            x = qs * pltpu.roll(kk, delta, 0) * jnp.exp(d_acc)
            v_s = pltpu.roll(vv, delta, 0)
        parts = []
        for h in range(N_HEADS):
            sl = slice(h * HEAD_DIM, (h + 1) * HEAD_DIM)
            w = jnp.sum(x[:, sl], axis=-1, keepdims=True)
            parts.append(jnp.where(row_in_sub >= delta, w, 0.0) * v_s[:, sl])
        o = o + jnp.concatenate(parts, axis=1)

    outs = []
    for h in range(N_HEADS):
        sl = slice(h * HEAD_DIM, (h + 1) * HEAD_DIM)
        st = st_ref[h]
        o_h = o[:, sl] + lax.dot_general(qe[:, sl], st.astype(BF16), NT_DIMS, preferred_element_type=F32)
        blocks = [jnp.zeros((S, HEAD_DIM), F32)]
        for sb in range(1, C // S):
            r0 = sb * S
            ref = bcs[r0 - 1:r0, sl]
            q_i = (qs[r0:r0 + S, sl] * jnp.exp(bcs[r0:r0 + S, sl] - ref)).astype(BF16)
            k_i = (kk[:, sl] * jnp.exp(jnp.where(row < r0, ref - bcs[:, sl], NEG_BIG))).astype(BF16)
            a_i = lax.dot_general(q_i, k_i, NT_DIMS, preferred_element_type=F32)
            blocks.append(jnp.dot(a_i.astype(BF16), vb[:, sl], preferred_element_type=F32))
        o_h = o_h + jnp.concatenate(blocks, axis=0)
        st_ref[h] = st * decay[:, sl] + jnp.dot(vv[:, sl].T.astype(BF16), kd[:, sl], preferred_element_type=F32)
        ms = jnp.mean(o_h * o_h, axis=-1, keepdims=True)
        outs.append(o_h * lax.rsqrt(ms + RMS_EPS))
    zg = g_ref[...].astype(F32)
    o_ref[...] = (jnp.concatenate(outs, axis=1) * ng_ref[...] * (zg * _sigmoid(zg))).astype(o_ref.dtype)


def _hgrn(h, batch, seq, lower_bound, norm_g):
    t = h.shape[0]
    C = HGRN_CHUNK
    n = seq // C
    tri = jnp.asarray(np.tril(np.ones((C, C), np.float32)), BF16)
    blk = lambda col: pl.BlockSpec((C, BRANCH_WIDTH), lambda b, c: (b * n + c, col))
    const = lambda *shape: pl.BlockSpec(shape, lambda b, c: (0,) * len(shape))
    return pl.pallas_call(
        _hgrn_kernel,
        grid=(batch, n),
        in_specs=[blk(1), blk(2), blk(3), blk(4), const(1, BRANCH_WIDTH), const(1, BRANCH_WIDTH), const(C, C)],
        out_specs=pl.BlockSpec((C, BRANCH_WIDTH), lambda b, c: (b * n + c, 0)),
        out_shape=jax.ShapeDtypeStruct((t, BRANCH_WIDTH), BF16),
        scratch_shapes=[pltpu.VMEM((N_HEADS, HEAD_DIM, HEAD_DIM), F32)],
        compiler_params=_params(2),
        name="hgrn2",
    )(h, h, h, h, lower_bound[None, :], jnp.tile(norm_g, N_HEADS)[None, :], tri)


def _fox_decay_kernel(f_ref, bias_ref, up_ref, low_ref, c_ref):
    z = f_ref[...] + bias_ref[...]
    ls = jnp.minimum(z, 0.0) - jnp.log(1.0 + jnp.exp(-jnp.abs(z)))
    within = _dot3_right(ls, up_ref[...])
    tot = jnp.broadcast_to(within[:, LANES - 1:LANES], within.shape)
    c_ref[...] = within + _dot3(low_ref[...], tot)


def _dot3_right(x, m):
    h1, h2, h3 = _split3(x)
    return (jnp.dot(h1, m, preferred_element_type=F32) + jnp.dot(h2, m, preferred_element_type=F32)
            + jnp.dot(h3, m, preferred_element_type=F32))


def _fox_decay(ffp, b_fox_f, batch, seq):
    nrow = seq // LANES
    f = ffp[:, :N_HEADS].reshape(batch, seq, N_HEADS).transpose(0, 2, 1).reshape(batch * N_HEADS * nrow, LANES)
    rows = f.shape[0]
    bias = jnp.broadcast_to(jnp.tile(jnp.repeat(b_fox_f, nrow), batch)[:, None], (rows, LANES))
    up = jnp.asarray(np.triu(np.ones((LANES, LANES), np.float32)), BF16)
    grp = np.arange(rows) // nrow
    low = (grp[:, None] == grp[None, :]) & (np.arange(rows)[None, :] < np.arange(rows)[:, None])
    low = jnp.asarray(low.astype(np.float32), BF16)
    full = lambda *shape: pl.BlockSpec(shape, lambda: (0,) * len(shape))
    c = pl.pallas_call(
        _fox_decay_kernel,
        in_specs=[full(rows, LANES), full(rows, LANES), full(LANES, LANES), full(rows, rows)],
        out_specs=full(rows, LANES),
        out_shape=jax.ShapeDtypeStruct((rows, LANES), F32),
        name="fox_decay",
    )(f, bias, up, low)
    return c.reshape(batch * N_HEADS, seq)


def _fox_norm_kernel(q_ref, k_ref, o_ref):
    lane = lax.broadcasted_iota(jnp.int32, (1, LANES), 1)
    row = jnp.zeros((1, LANES), F32)
    for idx, ref in enumerate((q_ref, k_ref)):
        z = ref[...].astype(F32)
        for hd in range(N_HEADS):
            zh = z[:, hd * HEAD_DIM:(hd + 1) * HEAD_DIM]
            n2 = jnp.max(jnp.sum(zh * zh, axis=-1, keepdims=True), axis=0, keepdims=True)
            row = jnp.where(lane == idx * N_HEADS + hd, n2, row)
    o_ref[0] = row


def _fox_needed_blocks(h, c, batch, seq, blk):
    nb = seq // blk
    norms = pl.pallas_call(
        _fox_norm_kernel,
        grid=(batch * nb,),
        in_specs=[pl.BlockSpec((blk, BRANCH_WIDTH), lambda i: (i, 7)),
                  pl.BlockSpec((blk, BRANCH_WIDTH), lambda i: (i, 8))],
        out_specs=pl.BlockSpec((1, 1, LANES), lambda i: (i, 0, 0)),
        out_shape=jax.ShapeDtypeStruct((batch * nb, 1, LANES), F32),
        compiler_params=_params(1),
        name="fox_norms",
    )(h, h)
    nr = jnp.sqrt(norms[:, 0, :2 * N_HEADS]).reshape(batch, nb, 2, N_HEADS)
    qn = nr[:, :, 0, :].transpose(0, 2, 1).reshape(batch * N_HEADS, nb)
    kn = nr[:, :, 1, :].transpose(0, 2, 1).reshape(batch * N_HEADS, nb)
    c_first = c[:, ::blk]
    c_last = c[:, blk - 1::blk]
    scale = FOX_BOUND_SLACK * HEAD_DIM ** -0.5
    upper = qn[:, :, None] * kn[:, None, :] * scale + c_first[:, :, None] - c_last[:, None, :]
    lower = -(qn * kn * scale)[:, :, None]
    diag = jnp.eye(nb, dtype=bool)[None]
    return diag | (upper - lower >= -FOX_UNDERFLOW)


def _fox_kernel(qmap_ref, kmap_ref, need_ref, keff_ref, q_ref, k_ref, v_ref, cq_ref, ck_ref, o_ref,
                m_ref, l_ref, acc_ref):
    bh, step = pl.program_id(0), pl.program_id(1)
    qi, ki = qmap_ref[step], kmap_ref[step]

    @pl.when(ki == 0)
    def _():
        m_ref[...] = jnp.full_like(m_ref, NEG_BIG)
        l_ref[...] = jnp.zeros_like(l_ref)
        acc_ref[...] = jnp.zeros_like(acc_ref)

    def update(diagonal):
        q = (q_ref[...].astype(F32) * (HEAD_DIM ** -0.5)).astype(BF16)
        s = lax.dot_general(q, k_ref[...], NT_DIMS, preferred_element_type=F32)
        s = s + (cq_ref[0, :, 0:1] - ck_ref[0])
        if diagonal:
            row = lax.broadcasted_iota(jnp.int32, s.shape, 0)
            col = lax.broadcasted_iota(jnp.int32, s.shape, 1)
            s = jnp.where(row >= col, s, NEG_BIG)
        m_prev = m_ref[...]
        m_new = jnp.maximum(m_prev, jnp.max(s, axis=-1, keepdims=True))
        alpha = jnp.exp(m_prev - m_new)
        p = jnp.exp(s - m_new)
        l_ref[...] = alpha * l_ref[...] + jnp.sum(p, axis=-1, keepdims=True)
        acc_ref[...] = alpha * acc_ref[...] + jnp.dot(p.astype(BF16), v_ref[...], preferred_element_type=F32)
        m_ref[...] = m_new

    @pl.when(jnp.logical_and(ki < qi, need_ref[bh, step] == 1))
    def _():
        update(False)

    @pl.when(ki == qi)
    def _():
        update(True)
        o_ref[...] = (acc_ref[...] / l_ref[...]).astype(o_ref.dtype)


def _fox(h, c, batch, seq, blk=1024):
    t = h.shape[0]
    nb = seq // blk
    c3 = c[:, None, :]
    col0 = 7 * BRANCH_WIDTH // HEAD_DIM
    pairs = [(qi, ki) for qi in range(nb) for ki in range(qi + 1)]
    qmap = jnp.asarray([p[0] for p in pairs], jnp.int32)
    kmap = jnp.asarray([p[1] for p in pairs], jnp.int32)
    needed = _fox_needed_blocks(h, c, batch, seq, blk)
    need_steps, key_steps = [], []
    for qi in range(nb):
        seg = needed[:, qi, :qi + 1]
        need_steps.append(seg)
        key_steps.append(lax.cummin(jnp.where(seg, jnp.arange(qi + 1, dtype=jnp.int32), qi), axis=1, reverse=True))
    need = jnp.concatenate(need_steps, axis=1).astype(jnp.int32)
    keff = jnp.concatenate(key_steps, axis=1).astype(jnp.int32)

    def qspec(off):
        return pl.BlockSpec((blk, HEAD_DIM),
                            lambda bh, s, qm, km, nd, ke: ((bh // N_HEADS) * nb + qm[s], col0 + off + bh % N_HEADS))

    def kspec(off):
        return pl.BlockSpec((blk, HEAD_DIM),
                            lambda bh, s, qm, km, nd, ke: ((bh // N_HEADS) * nb + ke[bh, s],
                                                           col0 + off + bh % N_HEADS))

    grid_spec = pltpu.PrefetchScalarGridSpec(
        num_scalar_prefetch=4,
        grid=(batch * N_HEADS, len(pairs)),
        in_specs=[
            qspec(0), kspec(N_HEADS), kspec(2 * N_HEADS),
            pl.BlockSpec((1, 1, blk), lambda bh, s, qm, km, nd, ke: (bh, 0, qm[s])),
            pl.BlockSpec((1, 1, blk), lambda bh, s, qm, km, nd, ke: (bh, 0, ke[bh, s])),
        ],
        out_specs=pl.BlockSpec((blk, HEAD_DIM),
                               lambda bh, s, qm, km, nd, ke: ((bh // N_HEADS) * nb + qm[s], bh % N_HEADS)),
        scratch_shapes=[pltpu.VMEM((blk, 1), F32), pltpu.VMEM((blk, 1), F32), pltpu.VMEM((blk, HEAD_DIM), F32)],
    )
    return pl.pallas_call(
        _fox_kernel,
        grid_spec=grid_spec,
        out_shape=jax.ShapeDtypeStruct((t, BRANCH_WIDTH), BF16),
        compiler_params=_params(2),
        name="fox_attention",
    )(qmap, kmap, need, keff, h, h, h, c3, c3)


def _merge_kernel(x_ref, a_ref, b_ref, c_ref, d_ref, wg_ref, bg_ref, wb_ref, wo_ref, g_ref, beta_ref,
                  o_ref, ob_ref, ot_ref, *, alpha):
    x = x_ref[...]
    xb = x.astype(BF16)
    merged = jnp.zeros(x.shape, F32)
    for m, br in enumerate((a_ref, b_ref, c_ref, d_ref)):
        sl = slice(m * D_MODEL, (m + 1) * D_MODEL)
        gate = _sigmoid(jnp.dot(xb, wg_ref[:, sl], preferred_element_type=F32) + bg_ref[:, sl])
        merged = merged + jnp.dot(br[...], wb_ref[m], preferred_element_type=F32) * gate
    mix = jnp.dot(merged.astype(BF16), wo_ref[...], preferred_element_type=F32)
    y = _layer_norm(alpha * x + mix, g_ref[...], beta_ref[...])
    o_ref[...] = y
    ob_ref[...] = y.astype(BF16)
    ot_ref[...] = y.T.astype(BF16)


def _merge(x, branches, w_gate, b_gate, w_branch, w_out, ln_g, ln_b, alpha, tm=256):
    t = x.shape[0]
    row = lambda width: pl.BlockSpec((tm, width), lambda i: (i, 0))
    const = lambda *shape: pl.BlockSpec(shape, lambda i: (0,) * len(shape))
    return pl.pallas_call(
        functools.partial(_merge_kernel, alpha=alpha),
        grid=(t // tm,),
        in_specs=[row(D_MODEL)] + [row(BRANCH_WIDTH)] * 4 + [
            const(D_MODEL, 4 * D_MODEL), const(1, 4 * D_MODEL), const(4, BRANCH_WIDTH, D_MODEL),
            const(D_MODEL, D_MODEL), const(1, D_MODEL), const(1, D_MODEL)],
        out_specs=[row(D_MODEL), row(D_MODEL), pl.BlockSpec((D_MODEL, tm), lambda i: (0, i))],
        out_shape=[jax.ShapeDtypeStruct((t, D_MODEL), F32), jax.ShapeDtypeStruct((t, D_MODEL), BF16),
                   jax.ShapeDtypeStruct((D_MODEL, t), BF16)],
        compiler_params=_params(1),
        name="merge_out_ln",
    )(x, *branches, w_gate.astype(BF16), b_gate[None, :], w_branch.astype(BF16), w_out.astype(BF16),
      ln_g[None, :], ln_b[None, :])


def _top_values(x, count, floor):
    tops = []
    for _ in range(count):
        m = jnp.max(x, axis=0, keepdims=True)
        tops.append(m)
        x = jnp.where(x >= m, floor, x)
    return tops


def _candidate_products(e1, e2):
    half = PEER_TOPK // 2
    sub = lax.broadcasted_iota(jnp.int32, (half, e1.shape[1]), 0)
    rows = [e1[0:1, :] * e2]
    for a in range(1, half):
        rows.append(jnp.where(sub < PEER_TOPK // (a + 1), e1[a:a + 1, :] * e2[0:half, :], -1.0))
    rows.append(e1[half:, :] * e2[0:1, :])
    return jnp.concatenate(rows, axis=0).astype(F32)


def _select_kernel(x_ref, wq_ref, k_ref, e1_ref, e2_ref, pt_ref):
    q = jnp.dot(x_ref[...], wq_ref[...], preferred_element_type=F32).astype(BF16)
    for h in range(PEER_HEADS):
        dense, top_e = [], []
        for half in range(2):
            c0 = (2 * h + half) * PEER_N_KEYS
            s = lax.dot_general(k_ref[half, h], q[:, c0:c0 + PEER_N_KEYS], NT_DIMS,
                                preferred_element_type=F32)
            tops = jnp.concatenate(_top_values(s, PEER_TOPK, -jnp.inf), axis=0)
            dense.append(jnp.where(s >= tops[PEER_TOPK - 1:, :], jnp.exp(s - tops[0:1, :]), 0.0))
            top_e.append(jnp.exp(tops - tops[0:1, :]))
        best = _top_values(_candidate_products(top_e[0], top_e[1]), PEER_TOPK, -1.0)
        z = best[0]
        for v in best[1:]:
            z = z + v
        inv_z = 0.5 / z
        e1n = (top_e[0] * inv_z).astype(BF16)
        e2b = top_e[1].astype(BF16)
        work = _candidate_products(e1n, e2b)
        count = jnp.zeros_like(inv_z)
        thr = jnp.zeros_like(inv_z)
        for _ in range(PEER_TOPK):
            m = jnp.max(work, axis=0, keepdims=True)
            hit = work >= m
            thr = jnp.where(count < PEER_TOPK, m, thr)
            count = count + jnp.sum(jnp.where(hit, 1.0, 0.0), axis=0, keepdims=True)
            work = jnp.where(hit, -1.0, work)
        e1_ref[:, h, :] = (dense[0] * inv_z).astype(BF16).astype(F32)
        e2_ref[h] = dense[1].astype(BF16)
        pt_ref[h:h + 1, :] = thr


def _peer_select(xb, wq, keys, tm=256):
    t = xb.shape[0]
    nq = wq.shape[1]
    const = lambda *shape: pl.BlockSpec(shape, lambda i: (0,) * len(shape))
    return pl.pallas_call(
        _select_kernel,
        grid=(t // tm,),
        in_specs=[pl.BlockSpec((tm, D_MODEL), lambda i: (i, 0)), const(D_MODEL, nq),
                  const(2, PEER_HEADS, PEER_N_KEYS, PEER_N_KEYS)],
        out_specs=[pl.BlockSpec((PEER_N_KEYS, PEER_HEADS, tm), lambda i: (0, 0, i)),
                   pl.BlockSpec((PEER_HEADS, PEER_N_KEYS, tm), lambda i: (0, 0, i)),
                   pl.BlockSpec((PEER_HEADS, tm), lambda i: (0, i))],
        out_shape=[jax.ShapeDtypeStruct((PEER_N_KEYS, PEER_HEADS, t), F32),
                   jax.ShapeDtypeStruct((PEER_HEADS, PEER_N_KEYS, t), BF16),
                   jax.ShapeDtypeStruct((PEER_HEADS, t), F32)],
        compiler_params=_params(1),
        name="peer_select",
    )(xb, wq.astype(BF16), keys.astype(BF16))


def _peer_dense_kernel(xt_ref, u_ref, v_ref, e1_ref, e2_ref, pt_ref, x_ref, g_ref, beta_ref, o_ref, ob_ref,
                       acc_ref, ht0_ref, ht1_ref, wt0_ref, wt1_ref, *, alpha, tb, te, n_tiles):
    s = pl.program_id(1)
    rows_per_step = te // PEER_N_KEYS

    @pl.when(s == 0)
    def _():
        acc_ref[...] = jnp.zeros_like(acc_ref)
        ht1_ref[...] = jnp.zeros_like(ht1_ref)
        wt0_ref[...] = jnp.zeros_like(wt0_ref)

    def stages(ht_w, ht_r, wt_w, wt_r):
        tile = jnp.clip(s - 1, 0, n_tiles - 1)

        def stage_a(k):
            ks = slice(k * (te // PEER_PIECES), (k + 1) * (te // PEER_PIECES))
            ht_w[ks, :] = jnp.dot(u_ref[ks, :], xt_ref[...], preferred_element_type=F32).astype(BF16)

        def stage_c(k):
            ks = slice(k * (tb // PEER_PIECES), (k + 1) * (tb // PEER_PIECES))
            acc_ref[ks, :] += lax.dot_general(wt_r[:, ks], v_ref[...], TN_DIMS, preferred_element_type=F32)

        def stage_b(rl, lt):
            r = tile * rows_per_step + rl
            ls = slice(lt * LANES, (lt + 1) * LANES)
            e1_rows = e1_ref[r, :, ls].astype(BF16)
            thr_rows = pt_ref[:, ls].astype(BF16)
            e1_b = [jnp.broadcast_to(e1_rows[h:h + 1, :], (PEER_SUB, LANES)) for h in range(PEER_HEADS)]
            thr_b = [jnp.broadcast_to(thr_rows[h:h + 1, :], (PEER_SUB, LANES)) for h in range(PEER_HEADS)]
            for sub in range(PEER_N_KEYS // PEER_SUB):
                js = slice(sub * PEER_SUB, (sub + 1) * PEER_SUB)
                rs = slice(rl * PEER_N_KEYS + sub * PEER_SUB, rl * PEER_N_KEYS + (sub + 1) * PEER_SUB)
                gate = jnp.zeros((PEER_SUB, LANES), BF16)
                for h in range(PEER_HEADS):
                    p = e1_b[h] * e2_ref[h, js, ls]
                    gate = gate + jnp.where(p >= thr_b[h], p, jnp.zeros_like(p))
                wt_w[rs, ls] = _gelu2(ht_r[rs, ls]) * gate

        units = [(rl, lt) for rl in range(rows_per_step) for lt in range(tb // LANES)]
        per_slot = len(units) // (2 * PEER_PIECES)
        for k in range(PEER_PIECES):
            stage_a(k)
            for rl, lt in units[2 * k * per_slot:(2 * k + 1) * per_slot]:
                stage_b(rl, lt)
            stage_c(k)
            for rl, lt in units[(2 * k + 1) * per_slot:(2 * k + 2) * per_slot]:
                stage_b(rl, lt)

    @pl.when(s % 2 == 0)
    def _():
        stages(ht0_ref, ht1_ref, wt1_ref, wt0_ref)

    @pl.when(s % 2 == 1)
    def _():
        stages(ht1_ref, ht0_ref, wt0_ref, wt1_ref)

    @pl.when(s == pl.num_programs(1) - 1)
    def _():
        y = _layer_norm(alpha * x_ref[...] + acc_ref[...], g_ref[...], beta_ref[...])
        o_ref[...] = y
        ob_ref[...] = y.astype(BF16)


def _peer_dense(x, xt, u_tab, v_tab, e1, e2, pt, ln_g, ln_b, alpha, tb=512, te=512):
    t = x.shape[0]
    n_tiles = u_tab.shape[0] // te
    assert n_tiles % 2 == 0
    const = lambda *shape: pl.BlockSpec(shape, lambda i, s: (0,) * len(shape))
    fac = pl.BlockSpec((PEER_HEADS, PEER_N_KEYS, tb), lambda i, s: (0, 0, i))
    row = pl.BlockSpec((tb, D_MODEL), lambda i, s: (i, 0))
    work = pltpu.VMEM((te, tb), BF16)
    return pl.pallas_call(
        functools.partial(_peer_dense_kernel, alpha=alpha, tb=tb, te=te, n_tiles=n_tiles),
        grid=(t // tb, n_tiles + 2),
        in_specs=[pl.BlockSpec((D_MODEL, tb), lambda i, s: (0, i)),
                  pl.BlockSpec((te, D_MODEL), lambda i, s: (jnp.minimum(s, n_tiles - 1), 0)),
                  pl.BlockSpec((te, D_MODEL), lambda i, s: (jnp.clip(s - 2, 0, n_tiles - 1), 0)),
                  pl.BlockSpec((PEER_N_KEYS, PEER_HEADS, tb), lambda i, s: (0, 0, i)), fac,
                  pl.BlockSpec((PEER_HEADS, tb), lambda i, s: (0, i)), row, const(1, D_MODEL), const(1, D_MODEL)],
        out_specs=[row, row],
        out_shape=[jax.ShapeDtypeStruct((t, D_MODEL), F32), jax.ShapeDtypeStruct((t, D_MODEL), BF16)],
        scratch_shapes=[pltpu.VMEM((tb, D_MODEL), F32), work, work, work, work],
        compiler_params=_params(2),
        name="peer_dense",
    )(xt, u_tab, v_tab, e1, e2, pt, x, ln_g[None, :], ln_b[None, :])


def kernel(x, w_in, b_fox_f, pool_w, pool_scale, hgrn_lb_logits, hgrn_norm_g, gmlp_ln_g, gmlp_ln_b, gmlp_ws, gmlp_bs,
           w_gate, b_gate, w_branch, w_out, ln1_g, ln1_b, peer_wq, peer_keys, peer_u, peer_v, ln2_g, ln2_b):
    batch, seq, d = x.shape
    depth = w_in.shape[0]
    alpha = (2.0 * depth) ** 0.25
    n_main = 10 * BRANCH_WIDTH

    p_lb = jax.nn.softmax(hgrn_lb_logits.astype(F32), axis=0)
    lower_bounds = jnp.cumsum(p_lb, axis=0) - p_lb[0]

    xf = x.reshape(batch * seq, d)
    xb = xf.astype(BF16)
    for l in range(depth):
        w_ff = jnp.pad(lax.slice(w_in, (l, 0, n_main), (l + 1, d, w_in.shape[2]))[0], ((0, 0), (0, LANES - N_HEADS)))
        h = _matmul(xb, w_in, BF16, tm=1024, tn=1024, n=n_main, layer=l)
        ffp = _matmul(xb, w_ff, F32, tm=2048, tn=LANES)

        a, c = _local_mixers(h, seq, pool_w[l], pool_scale[l], gmlp_ln_g[l], gmlp_ln_b[l], gmlp_ws[l], gmlp_bs[l])
        b = _hgrn(h, batch, seq, lower_bounds[l], hgrn_norm_g[l])
        cdec = _fox_decay(ffp, b_fox_f[l], batch, seq)
        dd = _fox(h, cdec, batch, seq)

        x1, x1b, x1t = _merge(xf, (a, b, c, dd), w_gate[l], b_gate[l], w_branch[l], w_out[l], ln1_g[l], ln1_b[l],
                              alpha)

        e1, e2, pt = _peer_select(x1b, peer_wq[l], peer_keys[l])
        xf, xb = _peer_dense(x1, x1t, peer_u[l].astype(BF16), peer_v[l].astype(BF16), e1, e2, pt,
                             ln2_g[l], ln2_b[l], alpha)
    return xf.reshape(batch, seq, d)
```

```python
import functools

import numpy as np
import jax
import jax.numpy as jnp
from jax import lax
from jax.experimental import pallas as pl
from jax.experimental.pallas import tpu as pltpu

F32 = jnp.float32
BF16 = jnp.bfloat16

D_MODEL = 1024
BRANCH_WIDTH = 512
HEAD_DIM = 128
N_HEADS = BRANCH_WIDTH // HEAD_DIM
POOL_WINDOWS = (2, 4, 8, 16)
POOL_HALO = 16
GMLP_CHUNK = 128
HGRN_CHUNK = 128
HGRN_SUB = 16
PEER_HEADS = 8
PEER_N_KEYS = 128
PEER_TOPK = 16
PEER_SUB = 32
PEER_PIECES = 4
LN_EPS = 1e-5
RMS_EPS = 1e-6
NEG_BIG = -1e30
FOX_UNDERFLOW = 110.0
FOX_BOUND_SLACK = 1.02

LANES = 128
SUBLANES = 8
VMEM_LIMIT = 48 * 1024 * 1024

NT_DIMS = (((1,), (1,)), ((), ()))
TN_DIMS = (((0,), (0,)), ((), ()))


def _params(n_grid, vmem=VMEM_LIMIT, flags=None):
    return pltpu.CompilerParams(dimension_semantics=("arbitrary",) * n_grid, vmem_limit_bytes=vmem, flags=flags)


def _gelu(x):
    return 0.5 * x * (1.0 + jnp.tanh(0.7978845608028654 * (x + 0.044715 * (x * x * x))))


def _gelu2(x):
    return x * (1.0 + jnp.tanh(x * (0.7978845608028654 + (0.7978845608028654 * 0.044715) * (x * x))))


def _sigmoid(x):
    return 1.0 / (1.0 + jnp.exp(-x))


def _split3(x):
    h1 = x.astype(BF16)
    r1 = x - h1.astype(F32)
    h2 = r1.astype(BF16)
    h3 = (r1 - h2.astype(F32)).astype(BF16)
    return h1, h2, h3


def _dot3(m, x):
    h1, h2, h3 = _split3(x)
    return (jnp.dot(m, h1, preferred_element_type=F32) + jnp.dot(m, h2, preferred_element_type=F32)
            + jnp.dot(m, h3, preferred_element_type=F32))


def _layer_norm(y, g, b):
    mu = jnp.mean(y, axis=-1, keepdims=True)
    yc = y - mu
    var = jnp.mean(yc * yc, axis=-1, keepdims=True)
    return yc * lax.rsqrt(var + LN_EPS) * g + b


def _mm_kernel(x_ref, w_ref, o_ref):
    o_ref[...] = jnp.dot(x_ref[...], w_ref[...].astype(BF16), preferred_element_type=F32).astype(o_ref.dtype)


def _matmul(x, w, out_dtype, tm, tn, n=None, layer=None):
    m, k = x.shape
    n = w.shape[-1] if n is None else n
    if layer is None:
        w_spec = pl.BlockSpec((k, tn), lambda i, j: (0, j))
    else:
        w_spec = pl.BlockSpec((None, k, tn), lambda i, j: (layer, 0, j))
    return pl.pallas_call(
        _mm_kernel,
        grid=(m // tm, n // tn),
        in_specs=[pl.BlockSpec((tm, k), lambda i, j: (i, 0)), w_spec],
        out_specs=pl.BlockSpec((tm, tn), lambda i, j: (i, j)),
        out_shape=jax.ShapeDtypeStruct((m, n), out_dtype),
        compiler_params=_params(2),
        name="proj_matmul",
    )(x, w)


def _local_kernel(p_ref, halo_ref, gu_ref, gv_ref, sm_ref, sh_ref, pw_ref, ps_ref, lg_ref, lb_ref, ws_ref,
                  bsb_ref, a_ref, c_ref, *, rows, tiles_per_seq):
    t_in_seq = pl.program_id(0) % tiles_per_seq
    p = p_ref[...]
    halo = halo_ref[...]
    halo = jnp.where(t_in_seq == 0, jnp.zeros_like(halo), halo)
    pos = t_in_seq * rows + lax.broadcasted_iota(jnp.int32, (rows, HEAD_DIM), 0)
    for g, w in enumerate(POOL_WINDOWS):
        sl = slice(g * HEAD_DIM, (g + 1) * HEAD_DIM)
        sums = (jnp.dot(sm_ref[g], p[:, sl], preferred_element_type=F32)
                + jnp.dot(sh_ref[g], halo[:, sl], preferred_element_type=F32))
        cnt = jnp.minimum(pos + 1, w).astype(F32)
        d = sums / cnt - p[:, sl].astype(F32)
        y = jnp.dot(d.astype(BF16), pw_ref[g], preferred_element_type=F32) * ps_ref[:, sl]
        a_ref[:, sl] = y.astype(a_ref.dtype)

    u = _gelu(gu_ref[...].astype(F32))
    v = _layer_norm(_gelu(gv_ref[...].astype(F32)), lg_ref[...], lb_ref[...]).astype(BF16)
    r = lax.broadcasted_iota(jnp.int32, (GMLP_CHUNK, GMLP_CHUNK), 0)
    c = lax.broadcasted_iota(jnp.int32, (GMLP_CHUNK, GMLP_CHUNK), 1)
    for g in range(N_HEADS):
        sl = slice(g * HEAD_DIM, (g + 1) * HEAD_DIM)
        wsc = jnp.where(r >= c, ws_ref[g], 0.0).astype(BF16)
        for ch in range(rows // GMLP_CHUNK):
            rs = slice(ch * GMLP_CHUNK, (ch + 1) * GMLP_CHUNK)
            s = jnp.dot(wsc, v[rs, sl], preferred_element_type=F32) + bsb_ref[:, sl]
            c_ref[rs, sl] = (u[rs, sl] * s).astype(c_ref.dtype)


def _pool_sum_matrices(rows):
    t = np.arange(rows)[:, None]
    s = np.arange(rows)[None, :]
    u = np.arange(POOL_HALO)[None, :]
    main = np.stack([((s <= t) & (s > t - w)) for w in POOL_WINDOWS]).astype(np.float32)
    halo = np.stack([(u >= t - w + POOL_HALO + 1) for w in POOL_WINDOWS]).astype(np.float32)
    return jnp.asarray(main, BF16), jnp.asarray(halo, BF16)


def _local_mixers(h, seq, pool_w, pool_scale, ln_g, ln_b, ws, bs, rows=256):
    t = h.shape[0]
    tiles_per_seq = seq // rows
    sm, sh = _pool_sum_matrices(rows)
    bsb = jnp.repeat(bs.T, HEAD_DIM, axis=1)
    halo_blocks = rows // POOL_HALO
    const = lambda *shape: pl.BlockSpec(shape, lambda i: (0,) * len(shape))
    return pl.pallas_call(
        functools.partial(_local_kernel, rows=rows, tiles_per_seq=tiles_per_seq),
        grid=(t // rows,),
        in_specs=[
            pl.BlockSpec((rows, BRANCH_WIDTH), lambda i: (i, 0)),
            pl.BlockSpec((POOL_HALO, BRANCH_WIDTH), lambda i: (jnp.maximum(i * halo_blocks - 1, 0), 0)),
            pl.BlockSpec((rows, BRANCH_WIDTH), lambda i: (i, 5)),
            pl.BlockSpec((rows, BRANCH_WIDTH), lambda i: (i, 6)),
            const(4, rows, rows), const(4, rows, POOL_HALO), const(4, HEAD_DIM, HEAD_DIM),
            const(1, BRANCH_WIDTH), const(1, BRANCH_WIDTH), const(1, BRANCH_WIDTH),
            const(4, GMLP_CHUNK, GMLP_CHUNK), const(GMLP_CHUNK, BRANCH_WIDTH),
        ],
        out_specs=[pl.BlockSpec((rows, BRANCH_WIDTH), lambda i: (i, 0))] * 2,
        out_shape=[jax.ShapeDtypeStruct((t, BRANCH_WIDTH), BF16)] * 2,
        compiler_params=_params(1),
        name="pool_gmlp",
    )(h, h, h, h, sm, sh, pool_w.astype(BF16), pool_scale[None, :], ln_g[None, :], ln_b[None, :], ws, bsb)


def _hgrn_kernel(q_ref, f_ref, i_ref, g_ref, lb_ref, ng_ref, tri_ref, o_ref, st_ref):
    C, S = HGRN_CHUNK, HGRN_SUB

    @pl.when(pl.program_id(1) == 0)
    def _():
        st_ref[...] = jnp.zeros_like(st_ref)

    zq = q_ref[...].astype(F32)
    qs = zq * _sigmoid(zq)
    kk = (1.0 - lb_ref[...]) * _sigmoid(-f_ref[...].astype(F32))
    logf = jnp.log(1.0 - kk)
    vv = i_ref[...].astype(F32)
    vb = vv.astype(BF16)
    bcs = _dot3(tri_ref[...], logf)
    b_last = bcs[C - 1:C, :]
    qe = (qs * jnp.exp(bcs)).astype(BF16)
    kd = (kk * jnp.exp(b_last - bcs)).astype(BF16)
    decay = jnp.exp(b_last)

    row = lax.broadcasted_iota(jnp.int32, (C, HEAD_DIM), 0)
    row_in_sub = row % S

    o = jnp.zeros((C, BRANCH_WIDTH), F32)
    d_acc = jnp.zeros((C, BRANCH_WIDTH), F32)
    for delta in range(S):
        if delta == 0:
            x = qs * kk
            v_s = vv
        else:
            d_acc = d_acc + pltpu.roll(logf, delta - 1, 0) if delta > 1 else d_acc + logf
            x = qs * pltpu.roll(kk, delta, 0) * jnp.exp(d_acc)
            v_s = pltpu.roll(vv, delta, 0)
        parts = []
        for h in range(N_HEADS):
            sl = slice(h * HEAD_DIM, (h + 1) * HEAD_DIM)
            w = jnp.sum(x[:, sl], axis=-1, keepdims=True)
            parts.append(jnp.where(row_in_sub >= delta, w, 0.0) * v_s[:, sl])
        o = o + jnp.concatenate(parts, axis=1)

    heads = [slice(h * HEAD_DIM, (h + 1) * HEAD_DIM) for h in range(N_HEADS)]
    states = [st_ref[h] for h in range(N_HEADS)]
    inter = [lax.dot_general(qe[:, sl], st.astype(BF16), NT_DIMS, preferred_element_type=F32)
             for sl, st in zip(heads, states)]
    weights = [[jnp.zeros((S, C), F32)] for _ in heads]
    for sb in range(1, C // S):
        r0 = sb * S
        for h, sl in enumerate(heads):
            ref = bcs[r0 - 1:r0, sl]
            q_i = (qs[r0:r0 + S, sl] * jnp.exp(bcs[r0:r0 + S, sl] - ref)).astype(BF16)
            k_i = (kk[:, sl] * jnp.exp(jnp.where(row < r0, ref - bcs[:, sl], NEG_BIG))).astype(BF16)
            weights[h].append(lax.dot_general(q_i, k_i, NT_DIMS, preferred_element_type=F32))
    outs = []
    for h, sl in enumerate(heads):
        a_h = jnp.concatenate(weights[h], axis=0).astype(BF16)
        o_h = o[:, sl] + inter[h] + jnp.dot(a_h, vb[:, sl], preferred_element_type=F32)
        st_ref[h] = states[h] * decay[:, sl] + jnp.dot(vv[:, sl].T.astype(BF16), kd[:, sl],
                                                       preferred_element_type=F32)
        ms = jnp.mean(o_h * o_h, axis=-1, keepdims=True)
        outs.append(o_h * lax.rsqrt(ms + RMS_EPS))
    zg = g_ref[...].astype(F32)
    o_ref[...] = (jnp.concatenate(outs, axis=1) * ng_ref[...] * (zg * _sigmoid(zg))).astype(o_ref.dtype)


def _hgrn(h, batch, seq, lower_bound, norm_g):
    t = h.shape[0]
    C = HGRN_CHUNK
    n = seq // C
    tri = jnp.asarray(np.tril(np.ones((C, C), np.float32)), BF16)
    blk = lambda col: pl.BlockSpec((C, BRANCH_WIDTH), lambda b, c: (b * n + c, col))
    const = lambda *shape: pl.BlockSpec(shape, lambda b, c: (0,) * len(shape))
    return pl.pallas_call(
        _hgrn_kernel,
        grid=(batch, n),
        in_specs=[blk(1), blk(2), blk(3), blk(4), const(1, BRANCH_WIDTH), const(1, BRANCH_WIDTH), const(C, C)],
        out_specs=pl.BlockSpec((C, BRANCH_WIDTH), lambda b, c: (b * n + c, 0)),
        out_shape=jax.ShapeDtypeStruct((t, BRANCH_WIDTH), BF16),
        scratch_shapes=[pltpu.VMEM((N_HEADS, HEAD_DIM, HEAD_DIM), F32)],
        compiler_params=_params(2),
        name="hgrn2",
    )(h, h, h, h, lower_bound[None, :], jnp.tile(norm_g, N_HEADS)[None, :], tri)


def _fox_decay_kernel(f_ref, bias_ref, up_ref, low_ref, c_ref):
    z = f_ref[...] + bias_ref[...]
    ls = jnp.minimum(z, 0.0) - jnp.log(1.0 + jnp.exp(-jnp.abs(z)))
    within = _dot3_right(ls, up_ref[...])
    tot = jnp.broadcast_to(within[:, LANES - 1:LANES], within.shape)
    c_ref[...] = within + _dot3(low_ref[...], tot)


def _dot3_right(x, m):
    h1, h2, h3 = _split3(x)
    return (jnp.dot(h1, m, preferred_element_type=F32) + jnp.dot(h2, m, preferred_element_type=F32)
            + jnp.dot(h3, m, preferred_element_type=F32))


def _fox_decay(ffp, b_fox_f, batch, seq):
    nrow = seq // LANES
    f = ffp[:, :N_HEADS].reshape(batch, seq, N_HEADS).transpose(0, 2, 1).reshape(batch * N_HEADS * nrow, LANES)
    rows = f.shape[0]
    bias = jnp.broadcast_to(jnp.tile(jnp.repeat(b_fox_f, nrow), batch)[:, None], (rows, LANES))
    up = jnp.asarray(np.triu(np.ones((LANES, LANES), np.float32)), BF16)
    grp = np.arange(rows) // nrow
    low = (grp[:, None] == grp[None, :]) & (np.arange(rows)[None, :] < np.arange(rows)[:, None])
    low = jnp.asarray(low.astype(np.float32), BF16)
    full = lambda *shape: pl.BlockSpec(shape, lambda: (0,) * len(shape))
    c = pl.pallas_call(
        _fox_decay_kernel,
        in_specs=[full(rows, LANES), full(rows, LANES), full(LANES, LANES), full(rows, rows)],
        out_specs=full(rows, LANES),
        out_shape=jax.ShapeDtypeStruct((rows, LANES), F32),
        name="fox_decay",
    )(f, bias, up, low)
    return c.reshape(batch * N_HEADS, seq)


def _fox_norm_kernel(q_ref, k_ref, o_ref):
    lane = lax.broadcasted_iota(jnp.int32, (1, LANES), 1)
    row = jnp.zeros((1, LANES), F32)
    for idx, ref in enumerate((q_ref, k_ref)):
        z = ref[...].astype(F32)
        for hd in range(N_HEADS):
            zh = z[:, hd * HEAD_DIM:(hd + 1) * HEAD_DIM]
            n2 = jnp.max(jnp.sum(zh * zh, axis=-1, keepdims=True), axis=0, keepdims=True)
            row = jnp.where(lane == idx * N_HEADS + hd, n2, row)
    o_ref[0] = row


def _fox_needed_blocks(h, c, batch, seq, blk):
    nb = seq // blk
    norms = pl.pallas_call(
        _fox_norm_kernel,
        grid=(batch * nb,),
        in_specs=[pl.BlockSpec((blk, BRANCH_WIDTH), lambda i: (i, 7)),
                  pl.BlockSpec((blk, BRANCH_WIDTH), lambda i: (i, 8))],
        out_specs=pl.BlockSpec((1, 1, LANES), lambda i: (i, 0, 0)),
        out_shape=jax.ShapeDtypeStruct((batch * nb, 1, LANES), F32),
        compiler_params=_params(1),
        name="fox_norms",
    )(h, h)
    nr = jnp.sqrt(norms[:, 0, :2 * N_HEADS]).reshape(batch, nb, 2, N_HEADS)
    qn = nr[:, :, 0, :].transpose(0, 2, 1).reshape(batch * N_HEADS, nb)
    kn = nr[:, :, 1, :].transpose(0, 2, 1).reshape(batch * N_HEADS, nb)
    c_first = c[:, ::blk]
    c_last = c[:, blk - 1::blk]
    scale = FOX_BOUND_SLACK * HEAD_DIM ** -0.5
    upper = qn[:, :, None] * kn[:, None, :] * scale + c_first[:, :, None] - c_last[:, None, :]
    lower = -(qn * kn * scale)[:, :, None]
    diag = jnp.eye(nb, dtype=bool)[None]
    return diag | (upper - lower >= -FOX_UNDERFLOW)


def _fox_kernel(qmap_ref, kmap_ref, need_ref, keff_ref, q_ref, k_ref, v_ref, cq_ref, ck_ref, o_ref,
                m_ref, l_ref, acc_ref):
    bh, step = pl.program_id(0), pl.program_id(1)
    qi, ki = qmap_ref[step], kmap_ref[step]

    @pl.when(ki == 0)
    def _():
        m_ref[...] = jnp.full_like(m_ref, NEG_BIG)
        l_ref[...] = jnp.zeros_like(l_ref)
        acc_ref[...] = jnp.zeros_like(acc_ref)

    def update(diagonal):
        q = (q_ref[...].astype(F32) * (HEAD_DIM ** -0.5)).astype(BF16)
        s = lax.dot_general(q, k_ref[...], NT_DIMS, preferred_element_type=F32)
        s = s + (cq_ref[0, :, 0:1] - ck_ref[0])
        if diagonal:
            row = lax.broadcasted_iota(jnp.int32, s.shape, 0)
            col = lax.broadcasted_iota(jnp.int32, s.shape, 1)
            s = jnp.where(row >= col, s, NEG_BIG)
        m_prev = m_ref[...]
        m_new = jnp.maximum(m_prev, jnp.max(s, axis=-1, keepdims=True))
        alpha = jnp.exp(m_prev - m_new)
        p = jnp.exp(s - m_new)
        l_ref[...] = alpha * l_ref[...] + jnp.sum(p, axis=-1, keepdims=True)
        acc_ref[...] = alpha * acc_ref[...] + jnp.dot(p.astype(BF16), v_ref[...], preferred_element_type=F32)
        m_ref[...] = m_new

    @pl.when(jnp.logical_and(ki < qi, need_ref[bh, step] == 1))
    def _():
        update(False)

    @pl.when(ki == qi)
    def _():
        update(True)
        o_ref[...] = (acc_ref[...] / l_ref[...]).astype(o_ref.dtype)


def _fox(h, c, batch, seq, blk=1024):
    t = h.shape[0]
    nb = seq // blk
    c3 = c[:, None, :]
    col0 = 7 * BRANCH_WIDTH // HEAD_DIM
    pairs = [(qi, ki) for qi in range(nb) for ki in range(qi + 1)]
    qmap = jnp.asarray([p[0] for p in pairs], jnp.int32)
    kmap = jnp.asarray([p[1] for p in pairs], jnp.int32)
    needed = _fox_needed_blocks(h, c, batch, seq, blk)
    need_steps, key_steps = [], []
    for qi in range(nb):
        seg = needed[:, qi, :qi + 1]
        need_steps.append(seg)
        key_steps.append(lax.cummin(jnp.where(seg, jnp.arange(qi + 1, dtype=jnp.int32), qi), axis=1, reverse=True))
    need = jnp.concatenate(need_steps, axis=1).astype(jnp.int32)
    keff = jnp.concatenate(key_steps, axis=1).astype(jnp.int32)

    def qspec(off):
        return pl.BlockSpec((blk, HEAD_DIM),
                            lambda bh, s, qm, km, nd, ke: ((bh // N_HEADS) * nb + qm[s], col0 + off + bh % N_HEADS))

    def kspec(off):
        return pl.BlockSpec((blk, HEAD_DIM),
                            lambda bh, s, qm, km, nd, ke: ((bh // N_HEADS) * nb + ke[bh, s],
                                                           col0 + off + bh % N_HEADS))

    grid_spec = pltpu.PrefetchScalarGridSpec(
        num_scalar_prefetch=4,
        grid=(batch * N_HEADS, len(pairs)),
        in_specs=[
            qspec(0), kspec(N_HEADS), kspec(2 * N_HEADS),
            pl.BlockSpec((1, 1, blk), lambda bh, s, qm, km, nd, ke: (bh, 0, qm[s])),
            pl.BlockSpec((1, 1, blk), lambda bh, s, qm, km, nd, ke: (bh, 0, ke[bh, s])),
        ],
        out_specs=pl.BlockSpec((blk, HEAD_DIM),
                               lambda bh, s, qm, km, nd, ke: ((bh // N_HEADS) * nb + qm[s], bh % N_HEADS)),
        scratch_shapes=[pltpu.VMEM((blk, 1), F32), pltpu.VMEM((blk, 1), F32), pltpu.VMEM((blk, HEAD_DIM), F32)],
    )
    return pl.pallas_call(
        _fox_kernel,
        grid_spec=grid_spec,
        out_shape=jax.ShapeDtypeStruct((t, BRANCH_WIDTH), BF16),
        compiler_params=_params(2),
        name="fox_attention",
    )(qmap, kmap, need, keff, h, h, h, c3, c3)


def _merge_kernel(x_ref, a_ref, b_ref, c_ref, d_ref, wg_ref, bg_ref, wb_ref, wo_ref, g_ref, beta_ref,
                  o_ref, ob_ref, ot_ref, *, alpha):
    x = x_ref[...]
    xb = x.astype(BF16)
    merged = jnp.zeros(x.shape, F32)
    for m, br in enumerate((a_ref, b_ref, c_ref, d_ref)):
        sl = slice(m * D_MODEL, (m + 1) * D_MODEL)
        gate = _sigmoid(jnp.dot(xb, wg_ref[:, sl], preferred_element_type=F32) + bg_ref[:, sl])
        merged = merged + jnp.dot(br[...], wb_ref[m], preferred_element_type=F32) * gate
    mix = jnp.dot(merged.astype(BF16), wo_ref[...], preferred_element_type=F32)
    y = _layer_norm(alpha * x + mix, g_ref[...], beta_ref[...])
    o_ref[...] = y
    ob_ref[...] = y.astype(BF16)
    ot_ref[...] = y.T.astype(BF16)


def _merge(x, branches, w_gate, b_gate, w_branch, w_out, ln_g, ln_b, alpha, tm=256):
    t = x.shape[0]
    row = lambda width: pl.BlockSpec((tm, width), lambda i: (i, 0))
    const = lambda *shape: pl.BlockSpec(shape, lambda i: (0,) * len(shape))
    return pl.pallas_call(
        functools.partial(_merge_kernel, alpha=alpha),
        grid=(t // tm,),
        in_specs=[row(D_MODEL)] + [row(BRANCH_WIDTH)] * 4 + [
            const(D_MODEL, 4 * D_MODEL), const(1, 4 * D_MODEL), const(4, BRANCH_WIDTH, D_MODEL),
            const(D_MODEL, D_MODEL), const(1, D_MODEL), const(1, D_MODEL)],
        out_specs=[row(D_MODEL), row(D_MODEL), pl.BlockSpec((D_MODEL, tm), lambda i: (0, i))],
        out_shape=[jax.ShapeDtypeStruct((t, D_MODEL), F32), jax.ShapeDtypeStruct((t, D_MODEL), BF16),
                   jax.ShapeDtypeStruct((D_MODEL, t), BF16)],
        compiler_params=_params(1),
        name="merge_out_ln",
    )(x, *branches, w_gate.astype(BF16), b_gate[None, :], w_branch.astype(BF16), w_out.astype(BF16),
      ln_g[None, :], ln_b[None, :])


def _sorting_network(n):
    pairs, p = [], 1
    while p < n:
        k = p
        while k >= 1:
            for j in range(k % p, n - k, 2 * k):
                for i in range(min(k, n - j - k)):
                    if (i + j) // (2 * p) == (i + j + k) // (2 * p):
                        pairs.append((i + j, i + j + k))
            k //= 2
        p *= 2
    return pairs


def _sorted_stack(s):
    v = [s[k * SUBLANES:(k + 1) * SUBLANES, :] for k in range(s.shape[0] // SUBLANES)]
    for i, j in _sorting_network(len(v)):
        v[i], v[j] = jnp.maximum(v[i], v[j]), jnp.minimum(v[i], v[j])
    return v


def _pop_largest(stacks, count, floor, with_hits=False):
    stacks = [list(g) for g in stacks]
    tops, hits = [], []
    for it in range(count):
        m = functools.reduce(jnp.maximum, [jnp.max(g[0], axis=0, keepdims=True) for g in stacks])
        tops.append(m)
        left = count - it - 1
        n_hit = 0.0
        for g in stacks:
            hit = g[0] >= m
            if with_hits:
                n_hit = n_hit + jnp.sum(jnp.where(hit, 1.0, 0.0), axis=0, keepdims=True)
            g[:] = [jnp.where(hit, g[k + 1] if k + 1 < len(g) else floor, g[k]) for k in range(min(len(g), left))]
        hits.append(n_hit)
    return (tops, hits) if with_hits else tops


def _candidate_stacks(e1, e2):
    sub = lax.broadcasted_iota(jnp.int32, (SUBLANES, e1.shape[1]), 0)
    prods = [(e1 * e2[b:b + 1, :]).astype(F32) for b in range(PEER_TOPK)]
    low = [jnp.where(sub < PEER_TOPK // (b + 1), prods[b][:SUBLANES, :], -1.0) for b in range(PEER_TOPK)]
    return [low, [prods[0][SUBLANES:, :]]]


def _select_kernel(x_ref, wq_ref, k_ref, e1_ref, e2_ref, pt_ref):
    q = jnp.dot(x_ref[...], wq_ref[...], preferred_element_type=F32).astype(BF16)
    for h in range(PEER_HEADS):
        dense, top_e = [], []
        for half in range(2):
            c0 = (2 * h + half) * PEER_N_KEYS
            s = lax.dot_general(k_ref[half, h], q[:, c0:c0 + PEER_N_KEYS], NT_DIMS,
                                preferred_element_type=F32)
            tops = jnp.concatenate(_pop_largest([_sorted_stack(s)], PEER_TOPK, -jnp.inf), axis=0)
            dense.append(jnp.where(s >= tops[PEER_TOPK - 1:, :], jnp.exp(s - tops[0:1, :]), 0.0))
            top_e.append(jnp.exp(tops - tops[0:1, :]))
        z = functools.reduce(jnp.add, _pop_largest(_candidate_stacks(top_e[0], top_e[1]), PEER_TOPK, -1.0))
        inv_z = 0.5 / z
        e1n = (top_e[0] * inv_z).astype(BF16)
        e2b = top_e[1].astype(BF16)
        best, repeats = _pop_largest(_candidate_stacks(e1n, e2b), PEER_TOPK, -1.0, with_hits=True)
        count = jnp.zeros_like(inv_z)
        thr = jnp.zeros_like(inv_z)
        for m, c in zip(best, repeats):
            thr = jnp.where(count < PEER_TOPK, m, thr)
            count = count + c
        e1_ref[:, h, :] = (dense[0] * inv_z).astype(BF16).astype(F32)
        e2_ref[h] = dense[1].astype(BF16)
        pt_ref[h:h + 1, :] = thr


def _peer_select(xb, wq, keys, tm=256):
    t = xb.shape[0]
    nq = wq.shape[1]
    const = lambda *shape: pl.BlockSpec(shape, lambda i: (0,) * len(shape))
    return pl.pallas_call(
        _select_kernel,
        grid=(t // tm,),
        in_specs=[pl.BlockSpec((tm, D_MODEL), lambda i: (i, 0)), const(D_MODEL, nq),
                  const(2, PEER_HEADS, PEER_N_KEYS, PEER_N_KEYS)],
        out_specs=[pl.BlockSpec((PEER_N_KEYS, PEER_HEADS, tm), lambda i: (0, 0, i)),
                   pl.BlockSpec((PEER_HEADS, PEER_N_KEYS, tm), lambda i: (0, 0, i)),
                   pl.BlockSpec((PEER_HEADS, tm), lambda i: (0, i))],
        out_shape=[jax.ShapeDtypeStruct((PEER_N_KEYS, PEER_HEADS, t), F32),
                   jax.ShapeDtypeStruct((PEER_HEADS, PEER_N_KEYS, t), BF16),
                   jax.ShapeDtypeStruct((PEER_HEADS, t), F32)],
        compiler_params=_params(1),
        name="peer_select",
    )(xb, wq.astype(BF16), keys.astype(BF16))


def _peer_dense_kernel(xt_ref, u_ref, v_ref, e1_ref, e2_ref, pt_ref, x_ref, g_ref, beta_ref, o_ref, ob_ref,
                       acc_ref, ht0_ref, ht1_ref, wt0_ref, wt1_ref, *, alpha, tb, te, n_tiles):
    s = pl.program_id(1)
    rows_per_step = te // PEER_N_KEYS

    @pl.when(s == 0)
    def _():
        acc_ref[...] = jnp.zeros_like(acc_ref)
        ht1_ref[...] = jnp.zeros_like(ht1_ref)
        wt0_ref[...] = jnp.zeros_like(wt0_ref)

    def stages(ht_w, ht_r, wt_w, wt_r):
        tile = jnp.clip(s - 1, 0, n_tiles - 1)

        def stage_a(k):
            ks = slice(k * (te // PEER_PIECES), (k + 1) * (te // PEER_PIECES))
            ht_w[ks, :] = jnp.dot(u_ref[ks, :], xt_ref[...], preferred_element_type=F32).astype(BF16)

        def stage_c(k):
            ks = slice(k * (tb // PEER_PIECES), (k + 1) * (tb // PEER_PIECES))
            acc_ref[ks, :] += lax.dot_general(wt_r[:, ks], v_ref[...], TN_DIMS, preferred_element_type=F32)

        def stage_b(rl, lt):
            r = tile * rows_per_step + rl
            ls = slice(lt * LANES, (lt + 1) * LANES)
            e1_rows = e1_ref[r, :, ls].astype(BF16)
            thr_rows = pt_ref[:, ls].astype(BF16)
            e1_b = [jnp.broadcast_to(e1_rows[h:h + 1, :], (PEER_SUB, LANES)) for h in range(PEER_HEADS)]
            thr_b = [jnp.broadcast_to(thr_rows[h:h + 1, :], (PEER_SUB, LANES)) for h in range(PEER_HEADS)]
            for sub in range(PEER_N_KEYS // PEER_SUB):
                js = slice(sub * PEER_SUB, (sub + 1) * PEER_SUB)
                rs = slice(rl * PEER_N_KEYS + sub * PEER_SUB, rl * PEER_N_KEYS + (sub + 1) * PEER_SUB)
                gate = jnp.zeros((PEER_SUB, LANES), BF16)
                for h in range(PEER_HEADS):
                    p = e1_b[h] * e2_ref[h, js, ls]
                    gate = gate + jnp.where(p >= thr_b[h], p, jnp.zeros_like(p))
                wt_w[rs, ls] = _gelu2(ht_r[rs, ls]) * gate

        units = [(rl, lt) for rl in range(rows_per_step) for lt in range(tb // LANES)]
        per_slot = len(units) // (2 * PEER_PIECES)
        for k in range(PEER_PIECES):
            stage_a(k)
            for rl, lt in units[2 * k * per_slot:(2 * k + 1) * per_slot]:
                stage_b(rl, lt)
            stage_c(k)
            for rl, lt in units[(2 * k + 1) * per_slot:(2 * k + 2) * per_slot]:
                stage_b(rl, lt)

    @pl.when(s % 2 == 0)
    def _():
        stages(ht0_ref, ht1_ref, wt1_ref, wt0_ref)

    @pl.when(s % 2 == 1)
    def _():
        stages(ht1_ref, ht0_ref, wt0_ref, wt1_ref)

    @pl.when(s == pl.num_programs(1) - 1)
    def _():
        y = _layer_norm(alpha * x_ref[...] + acc_ref[...], g_ref[...], beta_ref[...])
        o_ref[...] = y
        ob_ref[...] = y.astype(BF16)


def _peer_dense(x, xt, u_tab, v_tab, e1, e2, pt, ln_g, ln_b, alpha, tb=512, te=512):
    t = x.shape[0]
    n_tiles = u_tab.shape[0] // te
    assert n_tiles % 2 == 0
    const = lambda *shape: pl.BlockSpec(shape, lambda i, s: (0,) * len(shape))
    fac = pl.BlockSpec((PEER_HEADS, PEER_N_KEYS, tb), lambda i, s: (0, 0, i))
    row = pl.BlockSpec((tb, D_MODEL), lambda i, s: (i, 0))
    work = pltpu.VMEM((te, tb), BF16)
    return pl.pallas_call(
        functools.partial(_peer_dense_kernel, alpha=alpha, tb=tb, te=te, n_tiles=n_tiles),
        grid=(t // tb, n_tiles + 2),
        in_specs=[pl.BlockSpec((D_MODEL, tb), lambda i, s: (0, i)),
                  pl.BlockSpec((te, D_MODEL), lambda i, s: (jnp.minimum(s, n_tiles - 1), 0)),
                  pl.BlockSpec((te, D_MODEL), lambda i, s: (jnp.clip(s - 2, 0, n_tiles - 1), 0)),
                  pl.BlockSpec((PEER_N_KEYS, PEER_HEADS, tb), lambda i, s: (0, 0, i)), fac,
                  pl.BlockSpec((PEER_HEADS, tb), lambda i, s: (0, i)), row, const(1, D_MODEL), const(1, D_MODEL)],
        out_specs=[row, row],
        out_shape=[jax.ShapeDtypeStruct((t, D_MODEL), F32), jax.ShapeDtypeStruct((t, D_MODEL), BF16)],
        scratch_shapes=[pltpu.VMEM((tb, D_MODEL), F32), work, work, work, work],
        compiler_params=_params(2),
        name="peer_dense",
    )(xt, u_tab, v_tab, e1, e2, pt, x, ln_g[None, :], ln_b[None, :])


def kernel(x, w_in, b_fox_f, pool_w, pool_scale, hgrn_lb_logits, hgrn_norm_g, gmlp_ln_g, gmlp_ln_b, gmlp_ws, gmlp_bs,
           w_gate, b_gate, w_branch, w_out, ln1_g, ln1_b, peer_wq, peer_keys, peer_u, peer_v, ln2_g, ln2_b):
    batch, seq, d = x.shape
    depth = w_in.shape[0]
    alpha = (2.0 * depth) ** 0.25
    n_main = 10 * BRANCH_WIDTH

    p_lb = jax.nn.softmax(hgrn_lb_logits.astype(F32), axis=0)
    lower_bounds = jnp.cumsum(p_lb, axis=0) - p_lb[0]

    xf = x.reshape(batch * seq, d)
    xb = xf.astype(BF16)
    for l in range(depth):
        w_ff = jnp.pad(lax.slice(w_in, (l, 0, n_main), (l + 1, d, w_in.shape[2]))[0], ((0, 0), (0, LANES - N_HEADS)))
        h = _matmul(xb, w_in, BF16, tm=1024, tn=1024, n=n_main, layer=l)
        ffp = _matmul(xb, w_ff, F32, tm=2048, tn=LANES)

        a, c = _local_mixers(h, seq, pool_w[l], pool_scale[l], gmlp_ln_g[l], gmlp_ln_b[l], gmlp_ws[l], gmlp_bs[l])
        b = _hgrn(h, batch, seq, lower_bounds[l], hgrn_norm_g[l])
        cdec = _fox_decay(ffp, b_fox_f[l], batch, seq)
        dd = _fox(h, cdec, batch, seq)

        x1, x1b, x1t = _merge(xf, (a, b, c, dd), w_gate[l], b_gate[l], w_branch[l], w_out[l], ln1_g[l], ln1_b[l],
                              alpha)

        e1, e2, pt = _peer_select(x1b, peer_wq[l], peer_keys[l])
        xf, xb = _peer_dense(x1, x1t, peer_u[l].astype(BF16), peer_v[l].astype(BF16), e1, e2, pt,
                             ln2_g[l], ln2_b[l], alpha)
    return xf.reshape(batch, seq, d)
```

```python
import functools

import numpy as np
import jax
import jax.numpy as jnp
from jax import lax
from jax.experimental import pallas as pl
from jax.experimental.pallas import tpu as pltpu

F32 = jnp.float32
BF16 = jnp.bfloat16

D_MODEL = 1024
BRANCH_WIDTH = 512
HEAD_DIM = 128
N_HEADS = BRANCH_WIDTH // HEAD_DIM
POOL_WINDOWS = (2, 4, 8, 16)
POOL_HALO = 16
GMLP_CHUNK = 128
HGRN_CHUNK = 128
HGRN_SUB = 16
PEER_HEADS = 8
PEER_N_KEYS = 128
PEER_TOPK = 16
PEER_SUB = 32
PEER_PIECES = 2
LN_EPS = 1e-5
RMS_EPS = 1e-6
NEG_BIG = -1e30
FOX_UNDERFLOW = 110.0
FOX_BOUND_SLACK = 1.02

LANES = 128
SUBLANES = 8
VMEM_LIMIT = 48 * 1024 * 1024
PEER_DENSE_VMEM = 58 * 1024 * 1024

NT_DIMS = (((1,), (1,)), ((), ()))
TN_DIMS = (((0,), (0,)), ((), ()))


def _params(n_grid, vmem=VMEM_LIMIT, flags=None):
    return pltpu.CompilerParams(dimension_semantics=("arbitrary",) * n_grid, vmem_limit_bytes=vmem, flags=flags)


def _gelu(x):
    return 0.5 * x * (1.0 + jnp.tanh(0.7978845608028654 * (x + 0.044715 * (x * x * x))))


def _gelu2(x):
    return x * (1.0 + jnp.tanh(x * (0.7978845608028654 + (0.7978845608028654 * 0.044715) * (x * x))))


def _sigmoid(x):
    return 1.0 / (1.0 + jnp.exp(-x))


def _split3(x):
    h1 = x.astype(BF16)
    r1 = x - h1.astype(F32)
    h2 = r1.astype(BF16)
    h3 = (r1 - h2.astype(F32)).astype(BF16)
    return h1, h2, h3


def _dot3(m, x):
    h1, h2, h3 = _split3(x)
    return (jnp.dot(m, h1, preferred_element_type=F32) + jnp.dot(m, h2, preferred_element_type=F32)
            + jnp.dot(m, h3, preferred_element_type=F32))


def _layer_norm(y, g, b):
    mu = jnp.mean(y, axis=-1, keepdims=True)
    yc = y - mu
    var = jnp.mean(yc * yc, axis=-1, keepdims=True)
    return yc * lax.rsqrt(var + LN_EPS) * g + b


def _mm_kernel(x_ref, w_ref, o_ref):
    o_ref[...] = jnp.dot(x_ref[...], w_ref[...].astype(BF16), preferred_element_type=F32).astype(o_ref.dtype)


def _matmul(x, w, out_dtype, tm, tn, n=None, layer=None):
    m, k = x.shape
    n = w.shape[-1] if n is None else n
    if layer is None:
        w_spec = pl.BlockSpec((k, tn), lambda i, j: (0, j))
    else:
        w_spec = pl.BlockSpec((None, k, tn), lambda i, j: (layer, 0, j))
    return pl.pallas_call(
        _mm_kernel,
        grid=(m // tm, n // tn),
        in_specs=[pl.BlockSpec((tm, k), lambda i, j: (i, 0)), w_spec],
        out_specs=pl.BlockSpec((tm, tn), lambda i, j: (i, j)),
        out_shape=jax.ShapeDtypeStruct((m, n), out_dtype),
        compiler_params=_params(2),
        name="proj_matmul",
    )(x, w)


def _local_kernel(p_ref, halo_ref, gu_ref, gv_ref, sm_ref, sh_ref, pw_ref, ps_ref, lg_ref, lb_ref, ws_ref,
                  bsb_ref, a_ref, c_ref, *, rows, tiles_per_seq):
    t_in_seq = pl.program_id(0) % tiles_per_seq
    p = p_ref[...]
    halo = halo_ref[...]
    halo = jnp.where(t_in_seq == 0, jnp.zeros_like(halo), halo)
    pos = t_in_seq * rows + lax.broadcasted_iota(jnp.int32, (rows, HEAD_DIM), 0)
    for g, w in enumerate(POOL_WINDOWS):
        sl = slice(g * HEAD_DIM, (g + 1) * HEAD_DIM)
        sums = (jnp.dot(sm_ref[g], p[:, sl], preferred_element_type=F32)
                + jnp.dot(sh_ref[g], halo[:, sl], preferred_element_type=F32))
        cnt = jnp.minimum(pos + 1, w).astype(F32)
        d = sums / cnt - p[:, sl].astype(F32)
        y = jnp.dot(d.astype(BF16), pw_ref[g], preferred_element_type=F32) * ps_ref[:, sl]
        a_ref[:, sl] = y.astype(a_ref.dtype)

    u = _gelu(gu_ref[...].astype(F32))
    v = _layer_norm(_gelu(gv_ref[...].astype(F32)), lg_ref[...], lb_ref[...]).astype(BF16)
    r = lax.broadcasted_iota(jnp.int32, (GMLP_CHUNK, GMLP_CHUNK), 0)
    c = lax.broadcasted_iota(jnp.int32, (GMLP_CHUNK, GMLP_CHUNK), 1)
    for g in range(N_HEADS):
        sl = slice(g * HEAD_DIM, (g + 1) * HEAD_DIM)
        wsc = jnp.where(r >= c, ws_ref[g], 0.0).astype(BF16)
        for ch in range(rows // GMLP_CHUNK):
            rs = slice(ch * GMLP_CHUNK, (ch + 1) * GMLP_CHUNK)
            s = jnp.dot(wsc, v[rs, sl], preferred_element_type=F32) + bsb_ref[:, sl]
            c_ref[rs, sl] = (u[rs, sl] * s).astype(c_ref.dtype)


def _pool_sum_matrices(rows):
    t = np.arange(rows)[:, None]
    s = np.arange(rows)[None, :]
    u = np.arange(POOL_HALO)[None, :]
    main = np.stack([((s <= t) & (s > t - w)) for w in POOL_WINDOWS]).astype(np.float32)
    halo = np.stack([(u >= t - w + POOL_HALO + 1) for w in POOL_WINDOWS]).astype(np.float32)
    return jnp.asarray(main, BF16), jnp.asarray(halo, BF16)


def _local_mixers(h, seq, pool_w, pool_scale, ln_g, ln_b, ws, bs, rows=256):
    t = h.shape[0]
    tiles_per_seq = seq // rows
    sm, sh = _pool_sum_matrices(rows)
    bsb = jnp.repeat(bs.T, HEAD_DIM, axis=1)
    halo_blocks = rows // POOL_HALO
    const = lambda *shape: pl.BlockSpec(shape, lambda i: (0,) * len(shape))
    return pl.pallas_call(
        functools.partial(_local_kernel, rows=rows, tiles_per_seq=tiles_per_seq),
        grid=(t // rows,),
        in_specs=[
            pl.BlockSpec((rows, BRANCH_WIDTH), lambda i: (i, 0)),
            pl.BlockSpec((POOL_HALO, BRANCH_WIDTH), lambda i: (jnp.maximum(i * halo_blocks - 1, 0), 0)),
            pl.BlockSpec((rows, BRANCH_WIDTH), lambda i: (i, 5)),
            pl.BlockSpec((rows, BRANCH_WIDTH), lambda i: (i, 6)),
            const(4, rows, rows), const(4, rows, POOL_HALO), const(4, HEAD_DIM, HEAD_DIM),
            const(1, BRANCH_WIDTH), const(1, BRANCH_WIDTH), const(1, BRANCH_WIDTH),
            const(4, GMLP_CHUNK, GMLP_CHUNK), const(GMLP_CHUNK, BRANCH_WIDTH),
        ],
        out_specs=[pl.BlockSpec((rows, BRANCH_WIDTH), lambda i: (i, 0))] * 2,
        out_shape=[jax.ShapeDtypeStruct((t, BRANCH_WIDTH), BF16)] * 2,
        compiler_params=_params(1),
        name="pool_gmlp",
    )(h, h, h, h, sm, sh, pool_w.astype(BF16), pool_scale[None, :], ln_g[None, :], ln_b[None, :], ws, bsb)


def _hgrn_kernel(q_ref, f_ref, i_ref, g_ref, lb_ref, ng_ref, tri_ref, o_ref, st_ref):
    C, S = HGRN_CHUNK, HGRN_SUB

    @pl.when(pl.program_id(1) == 0)
    def _():
        st_ref[...] = jnp.zeros_like(st_ref)

    zq = q_ref[...].astype(F32)
    qs = zq * _sigmoid(zq)
    kk = (1.0 - lb_ref[...]) * _sigmoid(-f_ref[...].astype(F32))
    logf = jnp.log(1.0 - kk)
    vv = i_ref[...].astype(F32)
    vb = vv.astype(BF16)
    bcs = _dot3(tri_ref[...], logf)
    b_last = bcs[C - 1:C, :]
    qe = (qs * jnp.exp(bcs)).astype(BF16)
    kd = (kk * jnp.exp(b_last - bcs)).astype(BF16)
    decay = jnp.exp(b_last)

    row = lax.broadcasted_iota(jnp.int32, (C, HEAD_DIM), 0)
    row_in_sub = row % S

    o = jnp.zeros((C, BRANCH_WIDTH), F32)
    d_acc = jnp.zeros((C, BRANCH_WIDTH), F32)
    for delta in range(S):
        if delta == 0:
            x = qs * kk
            v_s = vv
        else:
            d_acc = d_acc + pltpu.roll(logf, delta - 1, 0) if delta > 1 else d_acc + logf
            x = qs * pltpu.roll(kk, delta, 0) * jnp.exp(d_acc)
            v_s = pltpu.roll(vv, delta, 0)
        parts = []
        for h in range(N_HEADS):
            sl = slice(h * HEAD_DIM, (h + 1) * HEAD_DIM)
            w = jnp.sum(x[:, sl], axis=-1, keepdims=True)
            parts.append(jnp.where(row_in_sub >= delta, w, 0.0) * v_s[:, sl])
        o = o + jnp.concatenate(parts, axis=1)

    heads = [slice(h * HEAD_DIM, (h + 1) * HEAD_DIM) for h in range(N_HEADS)]
    states = [st_ref[h] for h in range(N_HEADS)]
    inter = [lax.dot_general(qe[:, sl], st.astype(BF16), NT_DIMS, preferred_element_type=F32)
             for sl, st in zip(heads, states)]
    weights = [[jnp.zeros((S, C), F32)] for _ in heads]
    for sb in range(1, C // S):
        r0 = sb * S
        for h, sl in enumerate(heads):
            ref = bcs[r0 - 1:r0, sl]
            q_i = (qs[r0:r0 + S, sl] * jnp.exp(bcs[r0:r0 + S, sl] - ref)).astype(BF16)
            k_i = (kk[:, sl] * jnp.exp(jnp.where(row < r0, ref - bcs[:, sl], NEG_BIG))).astype(BF16)
            weights[h].append(lax.dot_general(q_i, k_i, NT_DIMS, preferred_element_type=F32))
    outs = []
    for h, sl in enumerate(heads):
        a_h = jnp.concatenate(weights[h], axis=0).astype(BF16)
        o_h = o[:, sl] + inter[h] + jnp.dot(a_h, vb[:, sl], preferred_element_type=F32)
        st_ref[h] = states[h] * decay[:, sl] + jnp.dot(vv[:, sl].T.astype(BF16), kd[:, sl],
                                                       preferred_element_type=F32)
        ms = jnp.mean(o_h * o_h, axis=-1, keepdims=True)
        outs.append(o_h * lax.rsqrt(ms + RMS_EPS))
    zg = g_ref[...].astype(F32)
    o_ref[...] = (jnp.concatenate(outs, axis=1) * ng_ref[...] * (zg * _sigmoid(zg))).astype(o_ref.dtype)


def _hgrn(h, batch, seq, lower_bound, norm_g):
    t = h.shape[0]
    C = HGRN_CHUNK
    n = seq // C
    tri = jnp.asarray(np.tril(np.ones((C, C), np.float32)), BF16)
    blk = lambda col: pl.BlockSpec((C, BRANCH_WIDTH), lambda b, c: (b * n + c, col))
    const = lambda *shape: pl.BlockSpec(shape, lambda b, c: (0,) * len(shape))
    return pl.pallas_call(
        _hgrn_kernel,
        grid=(batch, n),
        in_specs=[blk(1), blk(2), blk(3), blk(4), const(1, BRANCH_WIDTH), const(1, BRANCH_WIDTH), const(C, C)],
        out_specs=pl.BlockSpec((C, BRANCH_WIDTH), lambda b, c: (b * n + c, 0)),
        out_shape=jax.ShapeDtypeStruct((t, BRANCH_WIDTH), BF16),
        scratch_shapes=[pltpu.VMEM((N_HEADS, HEAD_DIM, HEAD_DIM), F32)],
        compiler_params=_params(2),
        name="hgrn2",
    )(h, h, h, h, lower_bound[None, :], jnp.tile(norm_g, N_HEADS)[None, :], tri)


def _fox_decay_kernel(f_ref, bias_ref, up_ref, low_ref, c_ref):
    z = f_ref[...] + bias_ref[...]
    ls = jnp.minimum(z, 0.0) - jnp.log(1.0 + jnp.exp(-jnp.abs(z)))
    within = _dot3_right(ls, up_ref[...])
    tot = jnp.broadcast_to(within[:, LANES - 1:LANES], within.shape)
    c_ref[...] = within + _dot3(low_ref[...], tot)


def _dot3_right(x, m):
    h1, h2, h3 = _split3(x)
    return (jnp.dot(h1, m, preferred_element_type=F32) + jnp.dot(h2, m, preferred_element_type=F32)
            + jnp.dot(h3, m, preferred_element_type=F32))


def _fox_decay(ffp, b_fox_f, batch, seq):
    nrow = seq // LANES
    f = ffp[:, :N_HEADS].reshape(batch, seq, N_HEADS).transpose(0, 2, 1).reshape(batch * N_HEADS * nrow, LANES)
    rows = f.shape[0]
    bias = jnp.broadcast_to(jnp.tile(jnp.repeat(b_fox_f, nrow), batch)[:, None], (rows, LANES))
    up = jnp.asarray(np.triu(np.ones((LANES, LANES), np.float32)), BF16)
    grp = np.arange(rows) // nrow
    low = (grp[:, None] == grp[None, :]) & (np.arange(rows)[None, :] < np.arange(rows)[:, None])
    low = jnp.asarray(low.astype(np.float32), BF16)
    full = lambda *shape: pl.BlockSpec(shape, lambda: (0,) * len(shape))
    c = pl.pallas_call(
        _fox_decay_kernel,
        in_specs=[full(rows, LANES), full(rows, LANES), full(LANES, LANES), full(rows, rows)],
        out_specs=full(rows, LANES),
        out_shape=jax.ShapeDtypeStruct((rows, LANES), F32),
        name="fox_decay",
    )(f, bias, up, low)
    return c.reshape(batch * N_HEADS, seq)


def _fox_norm_kernel(q_ref, k_ref, o_ref):
    lane = lax.broadcasted_iota(jnp.int32, (1, LANES), 1)
    row = jnp.zeros((1, LANES), F32)
    for idx, ref in enumerate((q_ref, k_ref)):
        z = ref[...].astype(F32)
        for hd in range(N_HEADS):
            zh = z[:, hd * HEAD_DIM:(hd + 1) * HEAD_DIM]
            n2 = jnp.max(jnp.sum(zh * zh, axis=-1, keepdims=True), axis=0, keepdims=True)
            row = jnp.where(lane == idx * N_HEADS + hd, n2, row)
    o_ref[0] = row


def _fox_needed_blocks(h, c, batch, seq, blk):
    nb = seq // blk
    norms = pl.pallas_call(
        _fox_norm_kernel,
        grid=(batch * nb,),
        in_specs=[pl.BlockSpec((blk, BRANCH_WIDTH), lambda i: (i, 7)),
                  pl.BlockSpec((blk, BRANCH_WIDTH), lambda i: (i, 8))],
        out_specs=pl.BlockSpec((1, 1, LANES), lambda i: (i, 0, 0)),
        out_shape=jax.ShapeDtypeStruct((batch * nb, 1, LANES), F32),
        compiler_params=_params(1),
        name="fox_norms",
    )(h, h)
    nr = jnp.sqrt(norms[:, 0, :2 * N_HEADS]).reshape(batch, nb, 2, N_HEADS)
    qn = nr[:, :, 0, :].transpose(0, 2, 1).reshape(batch * N_HEADS, nb)
    kn = nr[:, :, 1, :].transpose(0, 2, 1).reshape(batch * N_HEADS, nb)
    c_first = c[:, ::blk]
    c_last = c[:, blk - 1::blk]
    scale = FOX_BOUND_SLACK * HEAD_DIM ** -0.5
    upper = qn[:, :, None] * kn[:, None, :] * scale + c_first[:, :, None] - c_last[:, None, :]
    lower = -(qn * kn * scale)[:, :, None]
    diag = jnp.eye(nb, dtype=bool)[None]
    return diag | (upper - lower >= -FOX_UNDERFLOW)


def _fox_kernel(qmap_ref, kmap_ref, need_ref, keff_ref, q_ref, k_ref, v_ref, cq_ref, ck_ref, o_ref,
                m_ref, l_ref, acc_ref):
    bh, step = pl.program_id(0), pl.program_id(1)
    qi, ki = qmap_ref[step], kmap_ref[step]

    @pl.when(ki == 0)
    def _():
        m_ref[...] = jnp.full_like(m_ref, NEG_BIG)
        l_ref[...] = jnp.zeros_like(l_ref)
        acc_ref[...] = jnp.zeros_like(acc_ref)

    def update(diagonal):
        q = (q_ref[...].astype(F32) * (HEAD_DIM ** -0.5)).astype(BF16)
        s = lax.dot_general(q, k_ref[...], NT_DIMS, preferred_element_type=F32)
        s = s + (cq_ref[0, :, 0:1] - ck_ref[0])
        if diagonal:
            row = lax.broadcasted_iota(jnp.int32, s.shape, 0)
            col = lax.broadcasted_iota(jnp.int32, s.shape, 1)
            s = jnp.where(row >= col, s, NEG_BIG)
        m_prev = m_ref[...]
        m_new = jnp.maximum(m_prev, jnp.max(s, axis=-1, keepdims=True))
        alpha = jnp.exp(m_prev - m_new)
        p = jnp.exp(s - m_new)
        l_ref[...] = alpha * l_ref[...] + jnp.sum(p, axis=-1, keepdims=True)
        acc_ref[...] = alpha * acc_ref[...] + jnp.dot(p.astype(BF16), v_ref[...], preferred_element_type=F32)
        m_ref[...] = m_new

    @pl.when(jnp.logical_and(ki < qi, need_ref[bh, step] == 1))
    def _():
        update(False)

    @pl.when(ki == qi)
    def _():
        update(True)
        o_ref[...] = (acc_ref[...] / l_ref[...]).astype(o_ref.dtype)


def _fox(h, c, batch, seq, blk=1024):
    t = h.shape[0]
    nb = seq // blk
    c3 = c[:, None, :]
    col0 = 7 * BRANCH_WIDTH // HEAD_DIM
    pairs = [(qi, ki) for qi in range(nb) for ki in range(qi + 1)]
    qmap = jnp.asarray([p[0] for p in pairs], jnp.int32)
    kmap = jnp.asarray([p[1] for p in pairs], jnp.int32)
    needed = _fox_needed_blocks(h, c, batch, seq, blk)
    need_steps, key_steps = [], []
    for qi in range(nb):
        seg = needed[:, qi, :qi + 1]
        need_steps.append(seg)
        key_steps.append(lax.cummin(jnp.where(seg, jnp.arange(qi + 1, dtype=jnp.int32), qi), axis=1, reverse=True))
    need = jnp.concatenate(need_steps, axis=1).astype(jnp.int32)
    keff = jnp.concatenate(key_steps, axis=1).astype(jnp.int32)

    def qspec(off):
        return pl.BlockSpec((blk, HEAD_DIM),
                            lambda bh, s, qm, km, nd, ke: ((bh // N_HEADS) * nb + qm[s], col0 + off + bh % N_HEADS))

    def kspec(off):
        return pl.BlockSpec((blk, HEAD_DIM),
                            lambda bh, s, qm, km, nd, ke: ((bh // N_HEADS) * nb + ke[bh, s],
                                                           col0 + off + bh % N_HEADS))

    grid_spec = pltpu.PrefetchScalarGridSpec(
        num_scalar_prefetch=4,
        grid=(batch * N_HEADS, len(pairs)),
        in_specs=[
            qspec(0), kspec(N_HEADS), kspec(2 * N_HEADS),
            pl.BlockSpec((1, 1, blk), lambda bh, s, qm, km, nd, ke: (bh, 0, qm[s])),
            pl.BlockSpec((1, 1, blk), lambda bh, s, qm, km, nd, ke: (bh, 0, ke[bh, s])),
        ],
        out_specs=pl.BlockSpec((blk, HEAD_DIM),
                               lambda bh, s, qm, km, nd, ke: ((bh // N_HEADS) * nb + qm[s], bh % N_HEADS)),
        scratch_shapes=[pltpu.VMEM((blk, 1), F32), pltpu.VMEM((blk, 1), F32), pltpu.VMEM((blk, HEAD_DIM), F32)],
    )
    return pl.pallas_call(
        _fox_kernel,
        grid_spec=grid_spec,
        out_shape=jax.ShapeDtypeStruct((t, BRANCH_WIDTH), BF16),
        compiler_params=_params(2),
        name="fox_attention",
    )(qmap, kmap, need, keff, h, h, h, c3, c3)


def _merge_kernel(x_ref, a_ref, b_ref, c_ref, d_ref, wg_ref, bg_ref, wb_ref, wo_ref, g_ref, beta_ref,
                  o_ref, ob_ref, ot_ref, *, alpha):
    x = x_ref[...]
    xb = x.astype(BF16)
    merged = jnp.zeros(x.shape, F32)
    for m, br in enumerate((a_ref, b_ref, c_ref, d_ref)):
        sl = slice(m * D_MODEL, (m + 1) * D_MODEL)
        gate = _sigmoid(jnp.dot(xb, wg_ref[:, sl], preferred_element_type=F32) + bg_ref[:, sl])
        merged = merged + jnp.dot(br[...], wb_ref[m], preferred_element_type=F32) * gate
    mix = jnp.dot(merged.astype(BF16), wo_ref[...], preferred_element_type=F32)
    y = _layer_norm(alpha * x + mix, g_ref[...], beta_ref[...])
    o_ref[...] = y
    ob_ref[...] = y.astype(BF16)
    ot_ref[...] = y.T.astype(BF16)


def _merge(x, branches, w_gate, b_gate, w_branch, w_out, ln_g, ln_b, alpha, tm=256):
    t = x.shape[0]
    row = lambda width: pl.BlockSpec((tm, width), lambda i: (i, 0))
    const = lambda *shape: pl.BlockSpec(shape, lambda i: (0,) * len(shape))
    return pl.pallas_call(
        functools.partial(_merge_kernel, alpha=alpha),
        grid=(t // tm,),
        in_specs=[row(D_MODEL)] + [row(BRANCH_WIDTH)] * 4 + [
            const(D_MODEL, 4 * D_MODEL), const(1, 4 * D_MODEL), const(4, BRANCH_WIDTH, D_MODEL),
            const(D_MODEL, D_MODEL), const(1, D_MODEL), const(1, D_MODEL)],
        out_specs=[row(D_MODEL), row(D_MODEL), pl.BlockSpec((D_MODEL, tm), lambda i: (0, i))],
        out_shape=[jax.ShapeDtypeStruct((t, D_MODEL), F32), jax.ShapeDtypeStruct((t, D_MODEL), BF16),
                   jax.ShapeDtypeStruct((D_MODEL, t), BF16)],
        compiler_params=_params(1),
        name="merge_out_ln",
    )(x, *branches, w_gate.astype(BF16), b_gate[None, :], w_branch.astype(BF16), w_out.astype(BF16),
      ln_g[None, :], ln_b[None, :])


def _sorting_network(n):
    pairs, p = [], 1
    while p < n:
        k = p
        while k >= 1:
            for j in range(k % p, n - k, 2 * k):
                for i in range(min(k, n - j - k)):
                    if (i + j) // (2 * p) == (i + j + k) // (2 * p):
                        pairs.append((i + j, i + j + k))
            k //= 2
        p *= 2
    return pairs


def _sorted_stack(s):
    v = [s[k * SUBLANES:(k + 1) * SUBLANES, :] for k in range(s.shape[0] // SUBLANES)]
    for i, j in _sorting_network(len(v)):
        v[i], v[j] = jnp.maximum(v[i], v[j]), jnp.minimum(v[i], v[j])
    return v


def _pop_largest(stacks, count, floor, with_hits=False):
    stacks = [list(g) for g in stacks]
    tops, hits = [], []
    for it in range(count):
        m = functools.reduce(jnp.maximum, [jnp.max(g[0], axis=0, keepdims=True) for g in stacks])
        tops.append(m)
        left = count - it - 1
        n_hit = 0.0
        for g in stacks:
            hit = g[0] >= m
            if with_hits:
                n_hit = n_hit + jnp.sum(jnp.where(hit, 1.0, 0.0), axis=0, keepdims=True)
            g[:] = [jnp.where(hit, g[k + 1] if k + 1 < len(g) else floor, g[k]) for k in range(min(len(g), left))]
        hits.append(n_hit)
    return (tops, hits) if with_hits else tops


def _candidate_stacks(e1, e2):
    sub = lax.broadcasted_iota(jnp.int32, (SUBLANES, e1.shape[1]), 0)
    prods = [(e1 * e2[b:b + 1, :]).astype(F32) for b in range(PEER_TOPK)]
    low = [jnp.where(sub < PEER_TOPK // (b + 1), prods[b][:SUBLANES, :], -1.0) for b in range(PEER_TOPK)]
    return [low, [prods[0][SUBLANES:, :]]]


def _select_kernel(x_ref, wq_ref, k_ref, e1_ref, e2_ref, pt_ref):
    q = jnp.dot(x_ref[...], wq_ref[...], preferred_element_type=F32).astype(BF16)
    for h in range(PEER_HEADS):
        dense, top_e = [], []
        for half in range(2):
            c0 = (2 * h + half) * PEER_N_KEYS
            s = lax.dot_general(k_ref[half, h], q[:, c0:c0 + PEER_N_KEYS], NT_DIMS,
                                preferred_element_type=F32)
            tops = jnp.concatenate(_pop_largest([_sorted_stack(s)], PEER_TOPK, -jnp.inf), axis=0)
            dense.append(jnp.where(s >= tops[PEER_TOPK - 1:, :], jnp.exp(s - tops[0:1, :]), 0.0))
            top_e.append(jnp.exp(tops - tops[0:1, :]))
        z = functools.reduce(jnp.add, _pop_largest(_candidate_stacks(top_e[0], top_e[1]), PEER_TOPK, -1.0))
        inv_z = 0.5 / z
        e1n = (top_e[0] * inv_z).astype(BF16)
        e2b = top_e[1].astype(BF16)
        best, repeats = _pop_largest(_candidate_stacks(e1n, e2b), PEER_TOPK, -1.0, with_hits=True)
        count = jnp.zeros_like(inv_z)
        thr = jnp.zeros_like(inv_z)
        for m, c in zip(best, repeats):
            thr = jnp.where(count < PEER_TOPK, m, thr)
            count = count + c
        e1_ref[:, h, :] = (dense[0] * inv_z).astype(BF16).astype(F32)
        e2_ref[h] = dense[1].astype(BF16)
        pt_ref[h:h + 1, :] = thr


def _peer_select(xb, wq, keys, tm=256):
    t = xb.shape[0]
    nq = wq.shape[1]
    const = lambda *shape: pl.BlockSpec(shape, lambda i: (0,) * len(shape))
    return pl.pallas_call(
        _select_kernel,
        grid=(t // tm,),
        in_specs=[pl.BlockSpec((tm, D_MODEL), lambda i: (i, 0)), const(D_MODEL, nq),
                  const(2, PEER_HEADS, PEER_N_KEYS, PEER_N_KEYS)],
        out_specs=[pl.BlockSpec((PEER_N_KEYS, PEER_HEADS, tm), lambda i: (0, 0, i)),
                   pl.BlockSpec((PEER_HEADS, PEER_N_KEYS, tm), lambda i: (0, 0, i)),
                   pl.BlockSpec((PEER_HEADS, tm), lambda i: (0, i))],
        out_shape=[jax.ShapeDtypeStruct((PEER_N_KEYS, PEER_HEADS, t), F32),
                   jax.ShapeDtypeStruct((PEER_HEADS, PEER_N_KEYS, t), BF16),
                   jax.ShapeDtypeStruct((PEER_HEADS, t), F32)],
        compiler_params=_params(1),
        name="peer_select",
    )(xb, wq.astype(BF16), keys.astype(BF16))


def _peer_dense_kernel(xt_ref, u_ref, v_ref, e1_ref, e2_ref, pt_ref, x_ref, g_ref, beta_ref, o_ref, ob_ref,
                       acc_ref, ht0_ref, ht1_ref, wt0_ref, wt1_ref, *, alpha, tb, te, n_tiles, n_steps):
    s = pl.program_id(0)
    rows_per_step = te // PEER_N_KEYS
    c_tile = jnp.clip(s - 2, 0, n_steps - 1) % n_tiles

    @pl.when(s == 0)
    def _():
        ht1_ref[...] = jnp.zeros_like(ht1_ref)
        wt0_ref[...] = jnp.zeros_like(wt0_ref)

    @pl.when(c_tile == 0)
    def _():
        acc_ref[...] = jnp.zeros_like(acc_ref)

    def stages(ht_w, ht_r, wt_w, wt_r):
        tile = jnp.clip(s - 1, 0, n_steps - 1) % n_tiles

        def stage_a(k):
            ks = slice(k * (te // PEER_PIECES), (k + 1) * (te // PEER_PIECES))
            ht_w[ks, :] = jnp.dot(u_ref[ks, :], xt_ref[...], preferred_element_type=F32).astype(BF16)

        def stage_c(k):
            ks = slice(k * (tb // PEER_PIECES), (k + 1) * (tb // PEER_PIECES))
            acc_ref[ks, :] += lax.dot_general(wt_r[:, ks], v_ref[...], TN_DIMS, preferred_element_type=F32)

        def stage_b(rl, lt):
            r = tile * rows_per_step + rl
            ls = slice(lt * LANES, (lt + 1) * LANES)
            e1_rows = e1_ref[r, :, ls].astype(BF16)
            thr_rows = pt_ref[:, ls].astype(BF16)
            e1_b = [jnp.broadcast_to(e1_rows[h:h + 1, :], (PEER_SUB, LANES)) for h in range(PEER_HEADS)]
            thr_b = [jnp.broadcast_to(thr_rows[h:h + 1, :], (PEER_SUB, LANES)) for h in range(PEER_HEADS)]
            for sub in range(PEER_N_KEYS // PEER_SUB):
                js = slice(sub * PEER_SUB, (sub + 1) * PEER_SUB)
                rs = slice(rl * PEER_N_KEYS + sub * PEER_SUB, rl * PEER_N_KEYS + (sub + 1) * PEER_SUB)
                gate = jnp.zeros((PEER_SUB, LANES), BF16)
                for h in range(PEER_HEADS):
                    p = e1_b[h] * e2_ref[h, js, ls]
                    gate = gate + jnp.where(p >= thr_b[h], p, jnp.zeros_like(p))
                wt_w[rs, ls] = _gelu2(ht_r[rs, ls]) * gate

        units = [(rl, lt) for rl in range(rows_per_step) for lt in range(tb // LANES)]
        per_slot = len(units) // (2 * PEER_PIECES)
        for k in range(PEER_PIECES):
            stage_a(k)
            for rl, lt in units[2 * k * per_slot:(2 * k + 1) * per_slot]:
                stage_b(rl, lt)
            stage_c(k)
            for rl, lt in units[(2 * k + 1) * per_slot:(2 * k + 2) * per_slot]:
                stage_b(rl, lt)

    @pl.when(s % 2 == 0)
    def _():
        stages(ht0_ref, ht1_ref, wt1_ref, wt0_ref)

    @pl.when(s % 2 == 1)
    def _():
        stages(ht1_ref, ht0_ref, wt0_ref, wt1_ref)

    @pl.when(jnp.logical_and(s >= 2, c_tile == n_tiles - 1))
    def _():
        y = _layer_norm(alpha * x_ref[...] + acc_ref[...], g_ref[...], beta_ref[...])
        o_ref[...] = y
        ob_ref[...] = y.astype(BF16)


def _peer_dense(x, xt, u_tab, v_tab, layer, e1, e2, pt, ln_g, ln_b, alpha, tb=512, te=512):
    t = x.shape[0]
    n_tiles = u_tab.shape[1] // te
    n_steps = (t // tb) * n_tiles
    assert n_tiles % 2 == 0
    step_a = lambda s: jnp.minimum(s, n_steps - 1)
    step_b = lambda s: jnp.clip(s - 1, 0, n_steps - 1)
    step_c = lambda s: jnp.clip(s - 2, 0, n_steps - 1)
    const = lambda *shape: pl.BlockSpec(shape, lambda s: (0,) * len(shape))
    row_c = pl.BlockSpec((tb, D_MODEL), lambda s: (step_c(s) // n_tiles, 0))
    work = pltpu.VMEM((te, tb), BF16)
    return pl.pallas_call(
        functools.partial(_peer_dense_kernel, alpha=alpha, tb=tb, te=te, n_tiles=n_tiles, n_steps=n_steps),
        grid=(n_steps + 2,),
        in_specs=[pl.BlockSpec((D_MODEL, tb), lambda s: (0, step_a(s) // n_tiles)),
                  pl.BlockSpec((None, te, D_MODEL), lambda s: (layer, step_a(s) % n_tiles, 0)),
                  pl.BlockSpec((None, te, D_MODEL), lambda s: (layer, step_c(s) % n_tiles, 0)),
                  pl.BlockSpec((PEER_N_KEYS, PEER_HEADS, tb), lambda s: (0, 0, step_b(s) // n_tiles)),
                  pl.BlockSpec((PEER_HEADS, PEER_N_KEYS, tb), lambda s: (0, 0, step_b(s) // n_tiles)),
                  pl.BlockSpec((PEER_HEADS, tb), lambda s: (0, step_b(s) // n_tiles)),
                  row_c, const(1, D_MODEL), const(1, D_MODEL)],
        out_specs=[row_c, row_c],
        out_shape=[jax.ShapeDtypeStruct((t, D_MODEL), F32), jax.ShapeDtypeStruct((t, D_MODEL), BF16)],
        scratch_shapes=[pltpu.VMEM((tb, D_MODEL), F32), work, work, work, work],
        compiler_params=_params(1, vmem=PEER_DENSE_VMEM),
        name="peer_dense",
    )(xt, u_tab, v_tab, e1, e2, pt, x, ln_g[None, :], ln_b[None, :])


def kernel(x, w_in, b_fox_f, pool_w, pool_scale, hgrn_lb_logits, hgrn_norm_g, gmlp_ln_g, gmlp_ln_b, gmlp_ws, gmlp_bs,
           w_gate, b_gate, w_branch, w_out, ln1_g, ln1_b, peer_wq, peer_keys, peer_u, peer_v, ln2_g, ln2_b):
    batch, seq, d = x.shape
    depth = w_in.shape[0]
    alpha = (2.0 * depth) ** 0.25
    n_main = 10 * BRANCH_WIDTH

    p_lb = jax.nn.softmax(hgrn_lb_logits.astype(F32), axis=0)
    lower_bounds = jnp.cumsum(p_lb, axis=0) - p_lb[0]

    xf = x.reshape(batch * seq, d)
    xb = xf.astype(BF16)
    u_all, v_all = peer_u.astype(BF16), peer_v.astype(BF16)
    for l in range(depth):
        w_ff = jnp.pad(lax.slice(w_in, (l, 0, n_main), (l + 1, d, w_in.shape[2]))[0], ((0, 0), (0, LANES - N_HEADS)))
        h = _matmul(xb, w_in, BF16, tm=1024, tn=1024, n=n_main, layer=l)
        ffp = _matmul(xb, w_ff, F32, tm=2048, tn=LANES)

        a, c = _local_mixers(h, seq, pool_w[l], pool_scale[l], gmlp_ln_g[l], gmlp_ln_b[l], gmlp_ws[l], gmlp_bs[l])
        b = _hgrn(h, batch, seq, lower_bounds[l], hgrn_norm_g[l])
        cdec = _fox_decay(ffp, b_fox_f[l], batch, seq)
        dd = _fox(h, cdec, batch, seq)

        x1, x1b, x1t = _merge(xf, (a, b, c, dd), w_gate[l], b_gate[l], w_branch[l], w_out[l], ln1_g[l], ln1_b[l],
                              alpha)

        e1, e2, pt = _peer_select(x1b, peer_wq[l], peer_keys[l])
        xf, xb = _peer_dense(x1, x1t, u_all, v_all, l, e1, e2, pt, ln2_g[l], ln2_b[l], alpha)
    return xf.reshape(batch, seq, d)
```

```python
import functools

import numpy as np
import jax
import jax.numpy as jnp
from jax import lax
from jax.experimental import pallas as pl
from jax.experimental.pallas import tpu as pltpu

F32 = jnp.float32
BF16 = jnp.bfloat16

D_MODEL = 1024
BRANCH_WIDTH = 512
HEAD_DIM = 128
N_HEADS = BRANCH_WIDTH // HEAD_DIM
POOL_WINDOWS = (2, 4, 8, 16)
POOL_HALO = 16
GMLP_CHUNK = 128
HGRN_CHUNK = 128
HGRN_SUB = 16
PEER_HEADS = 8
PEER_N_KEYS = 128
PEER_TOPK = 16
PEER_SUB = 32
PEER_PIECES = 2
LN_EPS = 1e-5
RMS_EPS = 1e-6
NEG_BIG = -1e30
FOX_UNDERFLOW = 110.0
FOX_BOUND_SLACK = 1.02

LANES = 128
SUBLANES = 8
VMEM_LIMIT = 48 * 1024 * 1024
PEER_DENSE_VMEM = 58 * 1024 * 1024

NT_DIMS = (((1,), (1,)), ((), ()))
TN_DIMS = (((0,), (0,)), ((), ()))


def _params(n_grid, vmem=VMEM_LIMIT, flags=None):
    return pltpu.CompilerParams(dimension_semantics=("arbitrary",) * n_grid, vmem_limit_bytes=vmem, flags=flags)


def _gelu(x):
    return 0.5 * x * (1.0 + jnp.tanh(0.7978845608028654 * (x + 0.044715 * (x * x * x))))


def _gelu2(x):
    return x * (1.0 + jnp.tanh(x * (0.7978845608028654 + (0.7978845608028654 * 0.044715) * (x * x))))


def _sigmoid(x):
    return 1.0 / (1.0 + jnp.exp(-x))


def _split3(x):
    h1 = x.astype(BF16)
    r1 = x - h1.astype(F32)
    h2 = r1.astype(BF16)
    h3 = (r1 - h2.astype(F32)).astype(BF16)
    return h1, h2, h3


def _dot3(m, x):
    h1, h2, h3 = _split3(x)
    return (jnp.dot(m, h1, preferred_element_type=F32) + jnp.dot(m, h2, preferred_element_type=F32)
            + jnp.dot(m, h3, preferred_element_type=F32))


def _layer_norm(y, g, b):
    mu = jnp.mean(y, axis=-1, keepdims=True)
    yc = y - mu
    var = jnp.mean(yc * yc, axis=-1, keepdims=True)
    return yc * lax.rsqrt(var + LN_EPS) * g + b


def _mm_kernel(x_ref, w_ref, o_ref):
    o_ref[...] = jnp.dot(x_ref[...], w_ref[...].astype(BF16), preferred_element_type=F32).astype(o_ref.dtype)


def _matmul(x, w, out_dtype, tm, tn, n=None, layer=None):
    m, k = x.shape
    n = w.shape[-1] if n is None else n
    if layer is None:
        w_spec = pl.BlockSpec((k, tn), lambda i, j: (0, j))
    else:
        w_spec = pl.BlockSpec((None, k, tn), lambda i, j: (layer, 0, j))
    return pl.pallas_call(
        _mm_kernel,
        grid=(m // tm, n // tn),
        in_specs=[pl.BlockSpec((tm, k), lambda i, j: (i, 0)), w_spec],
        out_specs=pl.BlockSpec((tm, tn), lambda i, j: (i, j)),
        out_shape=jax.ShapeDtypeStruct((m, n), out_dtype),
        compiler_params=_params(2),
        name="proj_matmul",
    )(x, w)


def _local_kernel(p_ref, halo_ref, gu_ref, gv_ref, sm_ref, sh_ref, pw_ref, ps_ref, lg_ref, lb_ref, ws_ref,
                  bsb_ref, a_ref, c_ref, *, rows, tiles_per_seq):
    t_in_seq = pl.program_id(0) % tiles_per_seq
    p = p_ref[...]
    halo = halo_ref[...]
    halo = jnp.where(t_in_seq == 0, jnp.zeros_like(halo), halo)
    pos = t_in_seq * rows + lax.broadcasted_iota(jnp.int32, (rows, HEAD_DIM), 0)
    for g, w in enumerate(POOL_WINDOWS):
        sl = slice(g * HEAD_DIM, (g + 1) * HEAD_DIM)
        sums = (jnp.dot(sm_ref[g], p[:, sl], preferred_element_type=F32)
                + jnp.dot(sh_ref[g], halo[:, sl], preferred_element_type=F32))
        cnt = jnp.minimum(pos + 1, w).astype(F32)
        d = sums / cnt - p[:, sl].astype(F32)
        y = jnp.dot(d.astype(BF16), pw_ref[g], preferred_element_type=F32) * ps_ref[:, sl]
        a_ref[:, sl] = y.astype(a_ref.dtype)

    u = _gelu(gu_ref[...].astype(F32))
    v = _layer_norm(_gelu(gv_ref[...].astype(F32)), lg_ref[...], lb_ref[...]).astype(BF16)
    r = lax.broadcasted_iota(jnp.int32, (GMLP_CHUNK, GMLP_CHUNK), 0)
    c = lax.broadcasted_iota(jnp.int32, (GMLP_CHUNK, GMLP_CHUNK), 1)
    for g in range(N_HEADS):
        sl = slice(g * HEAD_DIM, (g + 1) * HEAD_DIM)
        wsc = jnp.where(r >= c, ws_ref[g], 0.0).astype(BF16)
        for ch in range(rows // GMLP_CHUNK):
            rs = slice(ch * GMLP_CHUNK, (ch + 1) * GMLP_CHUNK)
            s = jnp.dot(wsc, v[rs, sl], preferred_element_type=F32) + bsb_ref[:, sl]
            c_ref[rs, sl] = (u[rs, sl] * s).astype(c_ref.dtype)


def _pool_sum_matrices(rows):
    t = np.arange(rows)[:, None]
    s = np.arange(rows)[None, :]
    u = np.arange(POOL_HALO)[None, :]
    main = np.stack([((s <= t) & (s > t - w)) for w in POOL_WINDOWS]).astype(np.float32)
    halo = np.stack([(u >= t - w + POOL_HALO + 1) for w in POOL_WINDOWS]).astype(np.float32)
    return jnp.asarray(main, BF16), jnp.asarray(halo, BF16)


def _local_mixers(h, seq, pool_w, pool_scale, ln_g, ln_b, ws, bs, rows=256):
    t = h.shape[0]
    tiles_per_seq = seq // rows
    sm, sh = _pool_sum_matrices(rows)
    bsb = jnp.repeat(bs.T, HEAD_DIM, axis=1)
    halo_blocks = rows // POOL_HALO
    const = lambda *shape: pl.BlockSpec(shape, lambda i: (0,) * len(shape))
    return pl.pallas_call(
        functools.partial(_local_kernel, rows=rows, tiles_per_seq=tiles_per_seq),
        grid=(t // rows,),
        in_specs=[
            pl.BlockSpec((rows, BRANCH_WIDTH), lambda i: (i, 0)),
            pl.BlockSpec((POOL_HALO, BRANCH_WIDTH), lambda i: (jnp.maximum(i * halo_blocks - 1, 0), 0)),
            pl.BlockSpec((rows, BRANCH_WIDTH), lambda i: (i, 5)),
            pl.BlockSpec((rows, BRANCH_WIDTH), lambda i: (i, 6)),
            const(4, rows, rows), const(4, rows, POOL_HALO), const(4, HEAD_DIM, HEAD_DIM),
            const(1, BRANCH_WIDTH), const(1, BRANCH_WIDTH), const(1, BRANCH_WIDTH),
            const(4, GMLP_CHUNK, GMLP_CHUNK), const(GMLP_CHUNK, BRANCH_WIDTH),
        ],
        out_specs=[pl.BlockSpec((rows, BRANCH_WIDTH), lambda i: (i, 0))] * 2,
        out_shape=[jax.ShapeDtypeStruct((t, BRANCH_WIDTH), BF16)] * 2,
        compiler_params=_params(1),
        name="pool_gmlp",
    )(h, h, h, h, sm, sh, pool_w.astype(BF16), pool_scale[None, :], ln_g[None, :], ln_b[None, :], ws, bsb)


def _hgrn_kernel(q_ref, f_ref, i_ref, g_ref, lb_ref, ng_ref, tri_ref, o_ref, st_ref):
    C, S = HGRN_CHUNK, HGRN_SUB

    @pl.when(pl.program_id(1) == 0)
    def _():
        st_ref[...] = jnp.zeros_like(st_ref)

    zq = q_ref[...].astype(F32)
    qs = zq * _sigmoid(zq)
    kk = (1.0 - lb_ref[...]) * _sigmoid(-f_ref[...].astype(F32))
    logf = jnp.log(1.0 - kk)
    vv = i_ref[...].astype(F32)
    vb = vv.astype(BF16)
    bcs = _dot3(tri_ref[...], logf)
    b_last = bcs[C - 1:C, :]
    qe = (qs * jnp.exp(bcs)).astype(BF16)
    kd = (kk * jnp.exp(b_last - bcs)).astype(BF16)
    decay = jnp.exp(b_last)

    row = lax.broadcasted_iota(jnp.int32, (C, HEAD_DIM), 0)
    row_in_sub = row % S

    o = jnp.zeros((C, BRANCH_WIDTH), F32)
    d_acc = jnp.zeros((C, BRANCH_WIDTH), F32)
    for delta in range(S):
        if delta == 0:
            x = qs * kk
            v_s = vv
        else:
            d_acc = d_acc + pltpu.roll(logf, delta - 1, 0) if delta > 1 else d_acc + logf
            x = qs * pltpu.roll(kk, delta, 0) * jnp.exp(d_acc)
            v_s = pltpu.roll(vv, delta, 0)
        parts = []
        for h in range(N_HEADS):
            sl = slice(h * HEAD_DIM, (h + 1) * HEAD_DIM)
            w = jnp.sum(x[:, sl], axis=-1, keepdims=True)
            parts.append(jnp.where(row_in_sub >= delta, w, 0.0) * v_s[:, sl])
        o = o + jnp.concatenate(parts, axis=1)

    heads = [slice(h * HEAD_DIM, (h + 1) * HEAD_DIM) for h in range(N_HEADS)]
    states = [st_ref[h] for h in range(N_HEADS)]
    inter = [lax.dot_general(qe[:, sl], st.astype(BF16), NT_DIMS, preferred_element_type=F32)
             for sl, st in zip(heads, states)]
    weights = [[jnp.zeros((S, C), F32)] for _ in heads]
    for sb in range(1, C // S):
        r0 = sb * S
        for h, sl in enumerate(heads):
            ref = bcs[r0 - 1:r0, sl]
            q_i = (qs[r0:r0 + S, sl] * jnp.exp(bcs[r0:r0 + S, sl] - ref)).astype(BF16)
            k_i = (kk[:, sl] * jnp.exp(jnp.where(row < r0, ref - bcs[:, sl], NEG_BIG))).astype(BF16)
            weights[h].append(lax.dot_general(q_i, k_i, NT_DIMS, preferred_element_type=F32))
    outs = []
    for h, sl in enumerate(heads):
        a_h = jnp.concatenate(weights[h], axis=0).astype(BF16)
        o_h = o[:, sl] + inter[h] + jnp.dot(a_h, vb[:, sl], preferred_element_type=F32)
        st_ref[h] = states[h] * decay[:, sl] + jnp.dot(vv[:, sl].T.astype(BF16), kd[:, sl],
                                                       preferred_element_type=F32)
        ms = jnp.mean(o_h * o_h, axis=-1, keepdims=True)
        outs.append(o_h * lax.rsqrt(ms + RMS_EPS))
    zg = g_ref[...].astype(F32)
    o_ref[...] = (jnp.concatenate(outs, axis=1) * ng_ref[...] * (zg * _sigmoid(zg))).astype(o_ref.dtype)


def _hgrn(h, batch, seq, lower_bound, norm_g):
    t = h.shape[0]
    C = HGRN_CHUNK
    n = seq // C
    tri = jnp.asarray(np.tril(np.ones((C, C), np.float32)), BF16)
    blk = lambda col: pl.BlockSpec((C, BRANCH_WIDTH), lambda b, c: (b * n + c, col))
    const = lambda *shape: pl.BlockSpec(shape, lambda b, c: (0,) * len(shape))
    return pl.pallas_call(
        _hgrn_kernel,
        grid=(batch, n),
        in_specs=[blk(1), blk(2), blk(3), blk(4), const(1, BRANCH_WIDTH), const(1, BRANCH_WIDTH), const(C, C)],
        out_specs=pl.BlockSpec((C, BRANCH_WIDTH), lambda b, c: (b * n + c, 0)),
        out_shape=jax.ShapeDtypeStruct((t, BRANCH_WIDTH), BF16),
        scratch_shapes=[pltpu.VMEM((N_HEADS, HEAD_DIM, HEAD_DIM), F32)],
        compiler_params=_params(2),
        name="hgrn2",
    )(h, h, h, h, lower_bound[None, :], jnp.tile(norm_g, N_HEADS)[None, :], tri)


def _fox_decay_kernel(f_ref, bias_ref, up_ref, low_ref, c_ref):
    z = f_ref[...] + bias_ref[...]
    ls = jnp.minimum(z, 0.0) - jnp.log(1.0 + jnp.exp(-jnp.abs(z)))
    within = _dot3_right(ls, up_ref[...])
    tot = jnp.broadcast_to(within[:, LANES - 1:LANES], within.shape)
    c_ref[...] = within + _dot3(low_ref[...], tot)


def _dot3_right(x, m):
    h1, h2, h3 = _split3(x)
    return (jnp.dot(h1, m, preferred_element_type=F32) + jnp.dot(h2, m, preferred_element_type=F32)
            + jnp.dot(h3, m, preferred_element_type=F32))


def _fox_decay(ffp, b_fox_f, batch, seq):
    nrow = seq // LANES
    f = ffp[:, :N_HEADS].reshape(batch, seq, N_HEADS).transpose(0, 2, 1).reshape(batch * N_HEADS * nrow, LANES)
    rows = f.shape[0]
    bias = jnp.broadcast_to(jnp.tile(jnp.repeat(b_fox_f, nrow), batch)[:, None], (rows, LANES))
    up = jnp.asarray(np.triu(np.ones((LANES, LANES), np.float32)), BF16)
    grp = np.arange(rows) // nrow
    low = (grp[:, None] == grp[None, :]) & (np.arange(rows)[None, :] < np.arange(rows)[:, None])
    low = jnp.asarray(low.astype(np.float32), BF16)
    full = lambda *shape: pl.BlockSpec(shape, lambda: (0,) * len(shape))
    c = pl.pallas_call(
        _fox_decay_kernel,
        in_specs=[full(rows, LANES), full(rows, LANES), full(LANES, LANES), full(rows, rows)],
        out_specs=full(rows, LANES),
        out_shape=jax.ShapeDtypeStruct((rows, LANES), F32),
        name="fox_decay",
    )(f, bias, up, low)
    return c.reshape(batch * N_HEADS, seq)


def _fox_norm_kernel(q_ref, k_ref, o_ref):
    lane = lax.broadcasted_iota(jnp.int32, (1, LANES), 1)
    row = jnp.zeros((1, LANES), F32)
    for idx, ref in enumerate((q_ref, k_ref)):
        z = ref[...].astype(F32)
        for hd in range(N_HEADS):
            zh = z[:, hd * HEAD_DIM:(hd + 1) * HEAD_DIM]
            n2 = jnp.max(jnp.sum(zh * zh, axis=-1, keepdims=True), axis=0, keepdims=True)
            row = jnp.where(lane == idx * N_HEADS + hd, n2, row)
    o_ref[0] = row


def _fox_needed_blocks(h, c, batch, seq, blk):
    nb = seq // blk
    norms = pl.pallas_call(
        _fox_norm_kernel,
        grid=(batch * nb,),
        in_specs=[pl.BlockSpec((blk, BRANCH_WIDTH), lambda i: (i, 7)),
                  pl.BlockSpec((blk, BRANCH_WIDTH), lambda i: (i, 8))],
        out_specs=pl.BlockSpec((1, 1, LANES), lambda i: (i, 0, 0)),
        out_shape=jax.ShapeDtypeStruct((batch * nb, 1, LANES), F32),
        compiler_params=_params(1),
        name="fox_norms",
    )(h, h)
    nr = jnp.sqrt(norms[:, 0, :2 * N_HEADS]).reshape(batch, nb, 2, N_HEADS)
    qn = nr[:, :, 0, :].transpose(0, 2, 1).reshape(batch * N_HEADS, nb)
    kn = nr[:, :, 1, :].transpose(0, 2, 1).reshape(batch * N_HEADS, nb)
    c_first = c[:, ::blk]
    c_last = c[:, blk - 1::blk]
    scale = FOX_BOUND_SLACK * HEAD_DIM ** -0.5
    upper = qn[:, :, None] * kn[:, None, :] * scale + c_first[:, :, None] - c_last[:, None, :]
    lower = -(qn * kn * scale)[:, :, None]
    diag = jnp.eye(nb, dtype=bool)[None]
    return diag | (upper - lower >= -FOX_UNDERFLOW)


def _fox_kernel(qmap_ref, kmap_ref, need_ref, keff_ref, q_ref, k_ref, v_ref, cq_ref, ck_ref, o_ref,
                m_ref, l_ref, acc_ref):
    bh, step = pl.program_id(0), pl.program_id(1)
    qi, ki = qmap_ref[step], kmap_ref[step]

    @pl.when(ki == 0)
    def _():
        m_ref[...] = jnp.full_like(m_ref, NEG_BIG)
        l_ref[...] = jnp.zeros_like(l_ref)
        acc_ref[...] = jnp.zeros_like(acc_ref)

    def update(diagonal):
        q = (q_ref[...].astype(F32) * (HEAD_DIM ** -0.5)).astype(BF16)
        s = lax.dot_general(q, k_ref[...], NT_DIMS, preferred_element_type=F32)
        s = s + (cq_ref[0, :, 0:1] - ck_ref[0])
        if diagonal:
            row = lax.broadcasted_iota(jnp.int32, s.shape, 0)
            col = lax.broadcasted_iota(jnp.int32, s.shape, 1)
            s = jnp.where(row >= col, s, NEG_BIG)
        m_prev = m_ref[...]
        m_new = jnp.maximum(m_prev, jnp.max(s, axis=-1, keepdims=True))
        alpha = jnp.exp(m_prev - m_new)
        p = jnp.exp(s - m_new)
        l_ref[...] = alpha * l_ref[...] + jnp.sum(p, axis=-1, keepdims=True)
        acc_ref[...] = alpha * acc_ref[...] + jnp.dot(p.astype(BF16), v_ref[...], preferred_element_type=F32)
        m_ref[...] = m_new

    @pl.when(jnp.logical_and(ki < qi, need_ref[bh, step] == 1))
    def _():
        update(False)

    @pl.when(ki == qi)
    def _():
        update(True)
        o_ref[...] = (acc_ref[...] / l_ref[...]).astype(o_ref.dtype)


def _fox(h, c, batch, seq, blk=1024):
    t = h.shape[0]
    nb = seq // blk
    c3 = c[:, None, :]
    col0 = 7 * BRANCH_WIDTH // HEAD_DIM
    pairs = [(qi, ki) for qi in range(nb) for ki in range(qi + 1)]
    qmap = jnp.asarray([p[0] for p in pairs], jnp.int32)
    kmap = jnp.asarray([p[1] for p in pairs], jnp.int32)
    needed = _fox_needed_blocks(h, c, batch, seq, blk)
    need_steps, key_steps = [], []
    for qi in range(nb):
        seg = needed[:, qi, :qi + 1]
        need_steps.append(seg)
        key_steps.append(lax.cummin(jnp.where(seg, jnp.arange(qi + 1, dtype=jnp.int32), qi), axis=1, reverse=True))
    need = jnp.concatenate(need_steps, axis=1).astype(jnp.int32)
    keff = jnp.concatenate(key_steps, axis=1).astype(jnp.int32)

    def qspec(off):
        return pl.BlockSpec((blk, HEAD_DIM),
                            lambda bh, s, qm, km, nd, ke: ((bh // N_HEADS) * nb + qm[s], col0 + off + bh % N_HEADS))

    def kspec(off):
        return pl.BlockSpec((blk, HEAD_DIM),
                            lambda bh, s, qm, km, nd, ke: ((bh // N_HEADS) * nb + ke[bh, s],
                                                           col0 + off + bh % N_HEADS))

    grid_spec = pltpu.PrefetchScalarGridSpec(
        num_scalar_prefetch=4,
        grid=(batch * N_HEADS, len(pairs)),
        in_specs=[
            qspec(0), kspec(N_HEADS), kspec(2 * N_HEADS),
            pl.BlockSpec((1, 1, blk), lambda bh, s, qm, km, nd, ke: (bh, 0, qm[s])),
            pl.BlockSpec((1, 1, blk), lambda bh, s, qm, km, nd, ke: (bh, 0, ke[bh, s])),
        ],
        out_specs=pl.BlockSpec((blk, HEAD_DIM),
                               lambda bh, s, qm, km, nd, ke: ((bh // N_HEADS) * nb + qm[s], bh % N_HEADS)),
        scratch_shapes=[pltpu.VMEM((blk, 1), F32), pltpu.VMEM((blk, 1), F32), pltpu.VMEM((blk, HEAD_DIM), F32)],
    )
    return pl.pallas_call(
        _fox_kernel,
        grid_spec=grid_spec,
        out_shape=jax.ShapeDtypeStruct((t, BRANCH_WIDTH), BF16),
        compiler_params=_params(2),
        name="fox_attention",
    )(qmap, kmap, need, keff, h, h, h, c3, c3)


def _merge_kernel(x_ref, a_ref, b_ref, c_ref, d_ref, wg_ref, bg_ref, wb_ref, wo_ref, g_ref, beta_ref,
                  o_ref, ob_ref, ot_ref, *, alpha):
    x = x_ref[...]
    xb = x.astype(BF16)
    merged = jnp.zeros(x.shape, F32)
    for m, br in enumerate((a_ref, b_ref, c_ref, d_ref)):
        sl = slice(m * D_MODEL, (m + 1) * D_MODEL)
        gate = _sigmoid(jnp.dot(xb, wg_ref[:, sl], preferred_element_type=F32) + bg_ref[:, sl])
        merged = merged + jnp.dot(br[...], wb_ref[m], preferred_element_type=F32) * gate
    mix = jnp.dot(merged.astype(BF16), wo_ref[...], preferred_element_type=F32)
    y = _layer_norm(alpha * x + mix, g_ref[...], beta_ref[...])
    o_ref[...] = y
    ob_ref[...] = y.astype(BF16)
    ot_ref[...] = y.T.astype(BF16)


def _merge(x, branches, w_gate, b_gate, w_branch, w_out, ln_g, ln_b, alpha, tm=256):
    t = x.shape[0]
    row = lambda width: pl.BlockSpec((tm, width), lambda i: (i, 0))
    const = lambda *shape: pl.BlockSpec(shape, lambda i: (0,) * len(shape))
    return pl.pallas_call(
        functools.partial(_merge_kernel, alpha=alpha),
        grid=(t // tm,),
        in_specs=[row(D_MODEL)] + [row(BRANCH_WIDTH)] * 4 + [
            const(D_MODEL, 4 * D_MODEL), const(1, 4 * D_MODEL), const(4, BRANCH_WIDTH, D_MODEL),
            const(D_MODEL, D_MODEL), const(1, D_MODEL), const(1, D_MODEL)],
        out_specs=[row(D_MODEL), row(D_MODEL), pl.BlockSpec((D_MODEL, tm), lambda i: (0, i))],
        out_shape=[jax.ShapeDtypeStruct((t, D_MODEL), F32), jax.ShapeDtypeStruct((t, D_MODEL), BF16),
                   jax.ShapeDtypeStruct((D_MODEL, t), BF16)],
        compiler_params=_params(1),
        name="merge_out_ln",
    )(x, *branches, w_gate.astype(BF16), b_gate[None, :], w_branch.astype(BF16), w_out.astype(BF16),
      ln_g[None, :], ln_b[None, :])


def _sorting_network(n):
    pairs, p = [], 1
    while p < n:
        k = p
        while k >= 1:
            for j in range(k % p, n - k, 2 * k):
                for i in range(min(k, n - j - k)):
                    if (i + j) // (2 * p) == (i + j + k) // (2 * p):
                        pairs.append((i + j, i + j + k))
            k //= 2
        p *= 2
    return pairs


def _sorted_stack(s):
    v = [s[k * SUBLANES:(k + 1) * SUBLANES, :] for k in range(s.shape[0] // SUBLANES)]
    for i, j in _sorting_network(len(v)):
        v[i], v[j] = jnp.maximum(v[i], v[j]), jnp.minimum(v[i], v[j])
    return v


def _pop_largest(stacks, count, floor, with_hits=False):
    stacks = [list(g) for g in stacks]
    tops, hits = [], []
    for it in range(count):
        m = functools.reduce(jnp.maximum, [jnp.max(g[0], axis=0, keepdims=True) for g in stacks])
        tops.append(m)
        left = count - it - 1
        n_hit = 0.0
        for g in stacks:
            hit = g[0] >= m
            if with_hits:
                n_hit = n_hit + jnp.sum(jnp.where(hit, 1.0, 0.0), axis=0, keepdims=True)
            g[:] = [jnp.where(hit, g[k + 1] if k + 1 < len(g) else floor, g[k]) for k in range(min(len(g), left))]
        hits.append(n_hit)
    return (tops, hits) if with_hits else tops


def _candidate_stacks(e1, e2):
    sub = lax.broadcasted_iota(jnp.int32, (SUBLANES, e1.shape[1]), 0)
    prods = [(e1 * e2[b:b + 1, :]).astype(F32) for b in range(PEER_TOPK)]
    low = [jnp.where(sub < PEER_TOPK // (b + 1), prods[b][:SUBLANES, :], -1.0) for b in range(PEER_TOPK)]
    return [low, [prods[0][SUBLANES:, :]]]


def _select_kernel(x_ref, wq_ref, k_ref, e1_ref, e2_ref, pt_ref):
    q = jnp.dot(x_ref[...], wq_ref[...], preferred_element_type=F32).astype(BF16)
    for h in range(PEER_HEADS):
        dense, top_e = [], []
        for half in range(2):
            c0 = (2 * h + half) * PEER_N_KEYS
            s = lax.dot_general(k_ref[half, h], q[:, c0:c0 + PEER_N_KEYS], NT_DIMS,
                                preferred_element_type=F32)
            tops = jnp.concatenate(_pop_largest([_sorted_stack(s)], PEER_TOPK, -jnp.inf), axis=0)
            dense.append(jnp.where(s >= tops[PEER_TOPK - 1:, :], jnp.exp(s - tops[0:1, :]), 0.0))
            top_e.append(jnp.exp(tops - tops[0:1, :]))
        z = functools.reduce(jnp.add, _pop_largest(_candidate_stacks(top_e[0], top_e[1]), PEER_TOPK, -1.0))
        inv_z = 0.5 / z
        e1n = (top_e[0] * inv_z).astype(BF16)
        e2b = top_e[1].astype(BF16)
        best, repeats = _pop_largest(_candidate_stacks(e1n, e2b), PEER_TOPK, -1.0, with_hits=True)
        count = jnp.zeros_like(inv_z)
        thr = jnp.zeros_like(inv_z)
        for m, c in zip(best, repeats):
            thr = jnp.where(count < PEER_TOPK, m, thr)
            count = count + c
        e1_ref[:, h, :] = (dense[0] * inv_z).astype(BF16).astype(F32)
        e2_ref[h] = dense[1].astype(BF16)
        pt_ref[h:h + 1, :] = thr


def _peer_select(xb, wq, keys, tm=256):
    t = xb.shape[0]
    nq = wq.shape[1]
    const = lambda *shape: pl.BlockSpec(shape, lambda i: (0,) * len(shape))
    return pl.pallas_call(
        _select_kernel,
        grid=(t // tm,),
        in_specs=[pl.BlockSpec((tm, D_MODEL), lambda i: (i, 0)), const(D_MODEL, nq),
                  const(2, PEER_HEADS, PEER_N_KEYS, PEER_N_KEYS)],
        out_specs=[pl.BlockSpec((PEER_N_KEYS, PEER_HEADS, tm), lambda i: (0, 0, i)),
                   pl.BlockSpec((PEER_HEADS, PEER_N_KEYS, tm), lambda i: (0, 0, i)),
                   pl.BlockSpec((PEER_HEADS, tm), lambda i: (0, i))],
        out_shape=[jax.ShapeDtypeStruct((PEER_N_KEYS, PEER_HEADS, t), F32),
                   jax.ShapeDtypeStruct((PEER_HEADS, PEER_N_KEYS, t), BF16),
                   jax.ShapeDtypeStruct((PEER_HEADS, t), F32)],
        compiler_params=_params(1),
        name="peer_select",
    )(xb, wq.astype(BF16), keys.astype(BF16))


def _peer_dense_kernel(xt_ref, u_ref, v_ref, e1_ref, e2_ref, pt_ref, x_ref, g_ref, beta_ref, o_ref, ob_ref,
                       acc_ref, ht0_ref, ht1_ref, wt0_ref, wt1_ref, *, alpha, tb, te, n_tiles, n_steps):
    s = pl.program_id(0)
    rows_per_step = te // PEER_N_KEYS
    c_tile = jnp.clip(s - 2, 0, n_steps - 1) % n_tiles

    @pl.when(s == 0)
    def _():
        ht1_ref[...] = jnp.zeros_like(ht1_ref)
        wt0_ref[...] = jnp.zeros_like(wt0_ref)

    @pl.when(c_tile == 0)
    def _():
        acc_ref[...] = jnp.zeros_like(acc_ref)

    def stages(ht_w, ht_r, wt_w, wt_r):
        tile = jnp.clip(s - 1, 0, n_steps - 1) % n_tiles

        def stage_a(k):
            ks = slice(k * (te // PEER_PIECES), (k + 1) * (te // PEER_PIECES))
            ht_w[ks, :] = jnp.dot(u_ref[ks, :], xt_ref[...], preferred_element_type=F32).astype(BF16)

        def stage_c(k):
            ks = slice(k * (tb // PEER_PIECES), (k + 1) * (tb // PEER_PIECES))
            acc_ref[ks, :] += lax.dot_general(wt_r[:, ks], v_ref[...], TN_DIMS, preferred_element_type=F32)

        def stage_b(rl, lt):
            r = tile * rows_per_step + rl
            ls = slice(lt * LANES, (lt + 1) * LANES)
            e1_rows = e1_ref[r, :, ls].astype(BF16)
            thr_rows = pt_ref[:, ls].astype(BF16)
            e1_b = [jnp.broadcast_to(e1_rows[h:h + 1, :], (PEER_SUB, LANES)) for h in range(PEER_HEADS)]
            thr_b = [jnp.broadcast_to(thr_rows[h:h + 1, :], (PEER_SUB, LANES)) for h in range(PEER_HEADS)]
            for sub in range(PEER_N_KEYS // PEER_SUB):
                js = slice(sub * PEER_SUB, (sub + 1) * PEER_SUB)
                rs = slice(rl * PEER_N_KEYS + sub * PEER_SUB, rl * PEER_N_KEYS + (sub + 1) * PEER_SUB)
                gate = jnp.zeros((PEER_SUB, LANES), BF16)
                for h in range(PEER_HEADS):
                    p = e1_b[h] * e2_ref[h, js, ls]
                    gate = gate + jnp.where(p >= thr_b[h], p, jnp.zeros_like(p))
                wt_w[rs, ls] = _gelu2(ht_r[rs, ls]) * gate

        units = [(rl, lt) for rl in range(rows_per_step) for lt in range(tb // LANES)]
        per_slot = len(units) // (2 * PEER_PIECES)
        for k in range(PEER_PIECES):
            stage_a(k)
            for rl, lt in units[2 * k * per_slot:(2 * k + 1) * per_slot]:
                stage_b(rl, lt)
            stage_c(k)
            for rl, lt in units[(2 * k + 1) * per_slot:(2 * k + 2) * per_slot]:
                stage_b(rl, lt)

    @pl.when(s % 2 == 0)
    def _():
        stages(ht0_ref, ht1_ref, wt1_ref, wt0_ref)

    @pl.when(s % 2 == 1)
    def _():
        stages(ht1_ref, ht0_ref, wt0_ref, wt1_ref)

    @pl.when(jnp.logical_and(s >= 2, c_tile == n_tiles - 1))
    def _():
        y = _layer_norm(alpha * x_ref[...] + acc_ref[...], g_ref[...], beta_ref[...])
        o_ref[...] = y
        ob_ref[...] = y.astype(BF16)


def _peer_dense(x, xt, u_tab, v_tab, layer, e1, e2, pt, ln_g, ln_b, alpha, tb=1024, te=512):
    t = x.shape[0]
    n_tiles = u_tab.shape[1] // te
    n_steps = (t // tb) * n_tiles
    assert n_tiles % 2 == 0
    step_a = lambda s: jnp.minimum(s, n_steps - 1)
    step_b = lambda s: jnp.clip(s - 1, 0, n_steps - 1)
    step_c = lambda s: jnp.clip(s - 2, 0, n_steps - 1)
    const = lambda *shape: pl.BlockSpec(shape, lambda s: (0,) * len(shape))
    row_c = pl.BlockSpec((tb, D_MODEL), lambda s: (step_c(s) // n_tiles, 0))
    work = pltpu.VMEM((te, tb), BF16)
    return pl.pallas_call(
        functools.partial(_peer_dense_kernel, alpha=alpha, tb=tb, te=te, n_tiles=n_tiles, n_steps=n_steps),
        grid=(n_steps + 2,),
        in_specs=[pl.BlockSpec((D_MODEL, tb), lambda s: (0, step_a(s) // n_tiles)),
                  pl.BlockSpec((None, te, D_MODEL), lambda s: (layer, step_a(s) % n_tiles, 0)),
                  pl.BlockSpec((None, te, D_MODEL), lambda s: (layer, step_c(s) % n_tiles, 0)),
                  pl.BlockSpec((PEER_N_KEYS, PEER_HEADS, tb), lambda s: (0, 0, step_b(s) // n_tiles)),
                  pl.BlockSpec((PEER_HEADS, PEER_N_KEYS, tb), lambda s: (0, 0, step_b(s) // n_tiles)),
                  pl.BlockSpec((PEER_HEADS, tb), lambda s: (0, step_b(s) // n_tiles)),
                  row_c, const(1, D_MODEL), const(1, D_MODEL)],
        out_specs=[row_c, row_c],
        out_shape=[jax.ShapeDtypeStruct((t, D_MODEL), F32), jax.ShapeDtypeStruct((t, D_MODEL), BF16)],
        scratch_shapes=[pltpu.VMEM((tb, D_MODEL), F32), work, work, work, work],
        compiler_params=_params(1, vmem=PEER_DENSE_VMEM),
        name="peer_dense",
    )(xt, u_tab, v_tab, e1, e2, pt, x, ln_g[None, :], ln_b[None, :])


def kernel(x, w_in, b_fox_f, pool_w, pool_scale, hgrn_lb_logits, hgrn_norm_g, gmlp_ln_g, gmlp_ln_b, gmlp_ws, gmlp_bs,
           w_gate, b_gate, w_branch, w_out, ln1_g, ln1_b, peer_wq, peer_keys, peer_u, peer_v, ln2_g, ln2_b):
    batch, seq, d = x.shape
    depth = w_in.shape[0]
    alpha = (2.0 * depth) ** 0.25
    n_main = 10 * BRANCH_WIDTH

    p_lb = jax.nn.softmax(hgrn_lb_logits.astype(F32), axis=0)
    lower_bounds = jnp.cumsum(p_lb, axis=0) - p_lb[0]

    xf = x.reshape(batch * seq, d)
    xb = xf.astype(BF16)
    u_all, v_all = peer_u.astype(BF16), peer_v.astype(BF16)
    for l in range(depth):
        w_ff = jnp.pad(lax.slice(w_in, (l, 0, n_main), (l + 1, d, w_in.shape[2]))[0], ((0, 0), (0, LANES - N_HEADS)))
        h = _matmul(xb, w_in, BF16, tm=1024, tn=1024, n=n_main, layer=l)
        ffp = _matmul(xb, w_ff, F32, tm=2048, tn=LANES)

        a, c = _local_mixers(h, seq, pool_w[l], pool_scale[l], gmlp_ln_g[l], gmlp_ln_b[l], gmlp_ws[l], gmlp_bs[l])
        b = _hgrn(h, batch, seq, lower_bounds[l], hgrn_norm_g[l])
        cdec = _fox_decay(ffp, b_fox_f[l], batch, seq)
        dd = _fox(h, cdec, batch, seq)

        x1, x1b, x1t = _merge(xf, (a, b, c, dd), w_gate[l], b_gate[l], w_branch[l], w_out[l], ln1_g[l], ln1_b[l],
                              alpha)

        e1, e2, pt = _peer_select(x1b, peer_wq[l], peer_keys[l])
        xf, xb = _peer_dense(x1, x1t, u_all, v_all, l, e1, e2, pt, ln2_g[l], ln2_b[l], alpha)
    return xf.reshape(batch, seq, d)
```

```python
import functools

import numpy as np
import jax
import jax.numpy as jnp
from jax import lax
from jax.experimental import pallas as pl
from jax.experimental.pallas import tpu as pltpu

F32 = jnp.float32
BF16 = jnp.bfloat16

D_MODEL = 1024
BRANCH_WIDTH = 512
HEAD_DIM = 128
N_HEADS = BRANCH_WIDTH // HEAD_DIM
POOL_WINDOWS = (2, 4, 8, 16)
POOL_HALO = 16
GMLP_CHUNK = 128
HGRN_CHUNK = 128
HGRN_SUB = 16
PEER_HEADS = 8
PEER_N_KEYS = 128
PEER_TOPK = 16
PEER_SUB = 32
PEER_PIECES = 2
COL_POOL, COL_HQ, COL_HF, COL_HI, COL_HG, COL_GU, COL_GV, COL_FQ, COL_FK, COL_FV = range(10)
LN_EPS = 1e-5
RMS_EPS = 1e-6
NEG_BIG = -1e30
FOX_UNDERFLOW = 110.0
FOX_BOUND_SLACK = 1.02

LANES = 128
SUBLANES = 8
VMEM_LIMIT = 48 * 1024 * 1024
PEER_DENSE_VMEM = 58 * 1024 * 1024

NT_DIMS = (((1,), (1,)), ((), ()))
TN_DIMS = (((0,), (0,)), ((), ()))


def _params(n_grid, vmem=VMEM_LIMIT, flags=None):
    return pltpu.CompilerParams(dimension_semantics=("arbitrary",) * n_grid, vmem_limit_bytes=vmem, flags=flags)


def _gelu(x):
    return 0.5 * x * (1.0 + jnp.tanh(0.7978845608028654 * (x + 0.044715 * (x * x * x))))


def _gelu2(x):
    return x * (1.0 + jnp.tanh(x * (0.7978845608028654 + (0.7978845608028654 * 0.044715) * (x * x))))


def _sigmoid(x):
    return 1.0 / (1.0 + jnp.exp(-x))


def _split3(x):
    h1 = x.astype(BF16)
    r1 = x - h1.astype(F32)
    h2 = r1.astype(BF16)
    h3 = (r1 - h2.astype(F32)).astype(BF16)
    return h1, h2, h3


def _dot3(m, x):
    h1, h2, h3 = _split3(x)
    return (jnp.dot(m, h1, preferred_element_type=F32) + jnp.dot(m, h2, preferred_element_type=F32)
            + jnp.dot(m, h3, preferred_element_type=F32))


def _layer_norm(y, g, b):
    mu = jnp.mean(y, axis=-1, keepdims=True)
    yc = y - mu
    var = jnp.mean(yc * yc, axis=-1, keepdims=True)
    return yc * lax.rsqrt(var + LN_EPS) * g + b


def _mm_kernel(x_ref, w_ref, o_ref):
    o_ref[...] = jnp.dot(x_ref[...], w_ref[...].astype(BF16), preferred_element_type=F32).astype(o_ref.dtype)


def _matmul(x, w, out_dtype, tm, tn, n=None, layer=None):
    m, k = x.shape
    n = w.shape[-1] if n is None else n
    if layer is None:
        w_spec = pl.BlockSpec((k, tn), lambda i, j: (0, j))
    else:
        w_spec = pl.BlockSpec((None, k, tn), lambda i, j: (layer, 0, j))
    return pl.pallas_call(
        _mm_kernel,
        grid=(m // tm, n // tn),
        in_specs=[pl.BlockSpec((tm, k), lambda i, j: (i, 0)), w_spec],
        out_specs=pl.BlockSpec((tm, tn), lambda i, j: (i, j)),
        out_shape=jax.ShapeDtypeStruct((m, n), out_dtype),
        compiler_params=_params(2),
        name="proj_matmul",
    )(x, w)


def _local_kernel(p_ref, halo_ref, gu_ref, gv_ref, sm_ref, sh_ref, pw_ref, ps_ref, lg_ref, lb_ref, ws_ref,
                  bsb_ref, a_ref, c_ref, *, rows, tiles_per_seq):
    t_in_seq = pl.program_id(0) % tiles_per_seq
    p = p_ref[...]
    halo = halo_ref[...]
    halo = jnp.where(t_in_seq == 0, jnp.zeros_like(halo), halo)
    pos = t_in_seq * rows + lax.broadcasted_iota(jnp.int32, (rows, HEAD_DIM), 0)
    for g, w in enumerate(POOL_WINDOWS):
        sl = slice(g * HEAD_DIM, (g + 1) * HEAD_DIM)
        sums = (jnp.dot(sm_ref[g], p[:, sl], preferred_element_type=F32)
                + jnp.dot(sh_ref[g], halo[:, sl], preferred_element_type=F32))
        cnt = jnp.minimum(pos + 1, w).astype(F32)
        d = sums / cnt - p[:, sl].astype(F32)
        y = jnp.dot(d.astype(BF16), pw_ref[g], preferred_element_type=F32) * ps_ref[:, sl]
        a_ref[:, sl] = y.astype(a_ref.dtype)

    u = _gelu(gu_ref[...].astype(F32))
    v = _layer_norm(_gelu(gv_ref[...].astype(F32)), lg_ref[...], lb_ref[...]).astype(BF16)
    r = lax.broadcasted_iota(jnp.int32, (GMLP_CHUNK, GMLP_CHUNK), 0)
    c = lax.broadcasted_iota(jnp.int32, (GMLP_CHUNK, GMLP_CHUNK), 1)
    for g in range(N_HEADS):
        sl = slice(g * HEAD_DIM, (g + 1) * HEAD_DIM)
        wsc = jnp.where(r >= c, ws_ref[g], 0.0).astype(BF16)
        for ch in range(rows // GMLP_CHUNK):
            rs = slice(ch * GMLP_CHUNK, (ch + 1) * GMLP_CHUNK)
            s = jnp.dot(wsc, v[rs, sl], preferred_element_type=F32) + bsb_ref[:, sl]
            c_ref[rs, sl] = (u[rs, sl] * s).astype(c_ref.dtype)


def _pool_sum_matrices(rows):
    t = np.arange(rows)[:, None]
    s = np.arange(rows)[None, :]
    u = np.arange(POOL_HALO)[None, :]
    main = np.stack([((s <= t) & (s > t - w)) for w in POOL_WINDOWS]).astype(np.float32)
    halo = np.stack([(u >= t - w + POOL_HALO + 1) for w in POOL_WINDOWS]).astype(np.float32)
    return jnp.asarray(main, BF16), jnp.asarray(halo, BF16)


def _local_mixers(h, seq, pool_w, pool_scale, ln_g, ln_b, ws, bs, rows=256):
    t = h.shape[0]
    tiles_per_seq = seq // rows
    sm, sh = _pool_sum_matrices(rows)
    bsb = jnp.repeat(bs.T, HEAD_DIM, axis=1)
    halo_blocks = rows // POOL_HALO
    const = lambda *shape: pl.BlockSpec(shape, lambda i: (0,) * len(shape))
    return pl.pallas_call(
        functools.partial(_local_kernel, rows=rows, tiles_per_seq=tiles_per_seq),
        grid=(t // rows,),
        in_specs=[
            pl.BlockSpec((rows, BRANCH_WIDTH), lambda i: (i, 0)),
            pl.BlockSpec((POOL_HALO, BRANCH_WIDTH), lambda i: (jnp.maximum(i * halo_blocks - 1, 0), 0)),
            pl.BlockSpec((rows, BRANCH_WIDTH), lambda i: (i, COL_GU)),
            pl.BlockSpec((rows, BRANCH_WIDTH), lambda i: (i, COL_GV)),
            const(4, rows, rows), const(4, rows, POOL_HALO), const(4, HEAD_DIM, HEAD_DIM),
            const(1, BRANCH_WIDTH), const(1, BRANCH_WIDTH), const(1, BRANCH_WIDTH),
            const(4, GMLP_CHUNK, GMLP_CHUNK), const(GMLP_CHUNK, BRANCH_WIDTH),
        ],
        out_specs=[pl.BlockSpec((rows, BRANCH_WIDTH), lambda i: (i, 0))] * 2,
        out_shape=[jax.ShapeDtypeStruct((t, BRANCH_WIDTH), BF16)] * 2,
        compiler_params=_params(1),
        name="pool_gmlp",
    )(h, h, h, h, sm, sh, pool_w.astype(BF16), pool_scale[None, :], ln_g[None, :], ln_b[None, :], ws, bsb)


def _hgrn_kernel(q_ref, f_ref, i_ref, g_ref, lb_ref, ng_ref, tri_ref, o_ref, st_ref):
    C, S = HGRN_CHUNK, HGRN_SUB

    @pl.when(pl.program_id(1) == 0)
    def _():
        st_ref[...] = jnp.zeros_like(st_ref)

    zq = q_ref[...].astype(F32)
    qs = zq * _sigmoid(zq)
    kk = (1.0 - lb_ref[...]) * _sigmoid(-f_ref[...].astype(F32))
    logf = jnp.log(1.0 - kk)
    vv = i_ref[...].astype(F32)
    vb = vv.astype(BF16)
    bcs = _dot3(tri_ref[...], logf)
    b_last = bcs[C - 1:C, :]
    qe = (qs * jnp.exp(bcs)).astype(BF16)
    kd = (kk * jnp.exp(b_last - bcs)).astype(BF16)
    decay = jnp.exp(b_last)

    row = lax.broadcasted_iota(jnp.int32, (C, HEAD_DIM), 0)
    row_in_sub = row % S

    o = jnp.zeros((C, BRANCH_WIDTH), F32)
    d_acc = jnp.zeros((C, BRANCH_WIDTH), F32)
    for delta in range(S):
        if delta == 0:
            x = qs * kk
            v_s = vv
        else:
            d_acc = d_acc + pltpu.roll(logf, delta - 1, 0) if delta > 1 else d_acc + logf
            x = qs * pltpu.roll(kk, delta, 0) * jnp.exp(d_acc)
            v_s = pltpu.roll(vv, delta, 0)
        parts = []
        for h in range(N_HEADS):
            sl = slice(h * HEAD_DIM, (h + 1) * HEAD_DIM)
            w = jnp.sum(x[:, sl], axis=-1, keepdims=True)
            parts.append(jnp.where(row_in_sub >= delta, w, 0.0) * v_s[:, sl])
        o = o + jnp.concatenate(parts, axis=1)

    heads = [slice(h * HEAD_DIM, (h + 1) * HEAD_DIM) for h in range(N_HEADS)]
    states = [st_ref[h] for h in range(N_HEADS)]
    inter = [lax.dot_general(qe[:, sl], st.astype(BF16), NT_DIMS, preferred_element_type=F32)
             for sl, st in zip(heads, states)]
    weights = [[jnp.zeros((S, C), F32)] for _ in heads]
    for sb in range(1, C // S):
        r0 = sb * S
        for h, sl in enumerate(heads):
            ref = bcs[r0 - 1:r0, sl]
            q_i = (qs[r0:r0 + S, sl] * jnp.exp(bcs[r0:r0 + S, sl] - ref)).astype(BF16)
            k_i = (kk[:, sl] * jnp.exp(jnp.where(row < r0, ref - bcs[:, sl], NEG_BIG))).astype(BF16)
            weights[h].append(lax.dot_general(q_i, k_i, NT_DIMS, preferred_element_type=F32))
    outs = []
    for h, sl in enumerate(heads):
        a_h = jnp.concatenate(weights[h], axis=0).astype(BF16)
        o_h = o[:, sl] + inter[h] + jnp.dot(a_h, vb[:, sl], preferred_element_type=F32)
        st_ref[h] = states[h] * decay[:, sl] + jnp.dot(vv[:, sl].T.astype(BF16), kd[:, sl],
                                                       preferred_element_type=F32)
        ms = jnp.mean(o_h * o_h, axis=-1, keepdims=True)
        outs.append(o_h * lax.rsqrt(ms + RMS_EPS))
    zg = g_ref[...].astype(F32)
    o_ref[...] = (jnp.concatenate(outs, axis=1) * ng_ref[...] * (zg * _sigmoid(zg))).astype(o_ref.dtype)


def _hgrn(h, batch, seq, lower_bound, norm_g):
    t = h.shape[0]
    C = HGRN_CHUNK
    n = seq // C
    tri = jnp.asarray(np.tril(np.ones((C, C), np.float32)), BF16)
    blk = lambda col: pl.BlockSpec((C, BRANCH_WIDTH), lambda b, c: (b * n + c, col))
    const = lambda *shape: pl.BlockSpec(shape, lambda b, c: (0,) * len(shape))
    return pl.pallas_call(
        _hgrn_kernel,
        grid=(batch, n),
        in_specs=[blk(COL_HQ), blk(COL_HF), blk(COL_HI), blk(COL_HG), const(1, BRANCH_WIDTH), const(1, BRANCH_WIDTH), const(C, C)],
        out_specs=pl.BlockSpec((C, BRANCH_WIDTH), lambda b, c: (b * n + c, 0)),
        out_shape=jax.ShapeDtypeStruct((t, BRANCH_WIDTH), BF16),
        scratch_shapes=[pltpu.VMEM((N_HEADS, HEAD_DIM, HEAD_DIM), F32)],
        compiler_params=_params(2),
        name="hgrn2",
    )(h, h, h, h, lower_bound[None, :], jnp.tile(norm_g, N_HEADS)[None, :], tri)


def _fox_decay_kernel(f_ref, bias_ref, up_ref, low_ref, c_ref):
    z = f_ref[...] + bias_ref[...]
    ls = jnp.minimum(z, 0.0) - jnp.log(1.0 + jnp.exp(-jnp.abs(z)))
    within = _dot3_right(ls, up_ref[...])
    tot = jnp.broadcast_to(within[:, LANES - 1:LANES], within.shape)
    c_ref[...] = within + _dot3(low_ref[...], tot)


def _dot3_right(x, m):
    h1, h2, h3 = _split3(x)
    return (jnp.dot(h1, m, preferred_element_type=F32) + jnp.dot(h2, m, preferred_element_type=F32)
            + jnp.dot(h3, m, preferred_element_type=F32))


def _fox_decay(ffp, b_fox_f, batch, seq):
    nrow = seq // LANES
    f = ffp[:, :N_HEADS].reshape(batch, seq, N_HEADS).transpose(0, 2, 1).reshape(batch * N_HEADS * nrow, LANES)
    rows = f.shape[0]
    bias = jnp.broadcast_to(jnp.tile(jnp.repeat(b_fox_f, nrow), batch)[:, None], (rows, LANES))
    up = jnp.asarray(np.triu(np.ones((LANES, LANES), np.float32)), BF16)
    grp = np.arange(rows) // nrow
    low = (grp[:, None] == grp[None, :]) & (np.arange(rows)[None, :] < np.arange(rows)[:, None])
    low = jnp.asarray(low.astype(np.float32), BF16)
    full = lambda *shape: pl.BlockSpec(shape, lambda: (0,) * len(shape))
    c = pl.pallas_call(
        _fox_decay_kernel,
        in_specs=[full(rows, LANES), full(rows, LANES), full(LANES, LANES), full(rows, rows)],
        out_specs=full(rows, LANES),
        out_shape=jax.ShapeDtypeStruct((rows, LANES), F32),
        name="fox_decay",
    )(f, bias, up, low)
    return c.reshape(batch * N_HEADS, seq)


def _fox_norm_kernel(q_ref, k_ref, o_ref):
    lane = lax.broadcasted_iota(jnp.int32, (1, LANES), 1)
    row = jnp.zeros((1, LANES), F32)
    for idx, ref in enumerate((q_ref, k_ref)):
        z = ref[...].astype(F32)
        for hd in range(N_HEADS):
            zh = z[:, hd * HEAD_DIM:(hd + 1) * HEAD_DIM]
            n2 = jnp.max(jnp.sum(zh * zh, axis=-1, keepdims=True), axis=0, keepdims=True)
            row = jnp.where(lane == idx * N_HEADS + hd, n2, row)
    o_ref[0] = row


def _fox_needed_blocks(h, c, batch, seq, blk):
    nb = seq // blk
    norms = pl.pallas_call(
        _fox_norm_kernel,
        grid=(batch * nb,),
        in_specs=[pl.BlockSpec((blk, BRANCH_WIDTH), lambda i: (i, COL_FQ)),
                  pl.BlockSpec((blk, BRANCH_WIDTH), lambda i: (i, COL_FK))],
        out_specs=pl.BlockSpec((1, 1, LANES), lambda i: (i, 0, 0)),
        out_shape=jax.ShapeDtypeStruct((batch * nb, 1, LANES), F32),
        compiler_params=_params(1),
        name="fox_norms",
    )(h, h)
    nr = jnp.sqrt(norms[:, 0, :2 * N_HEADS]).reshape(batch, nb, 2, N_HEADS)
    qn = nr[:, :, 0, :].transpose(0, 2, 1).reshape(batch * N_HEADS, nb)
    kn = nr[:, :, 1, :].transpose(0, 2, 1).reshape(batch * N_HEADS, nb)
    c_first = c[:, ::blk]
    c_last = c[:, blk - 1::blk]
    scale = FOX_BOUND_SLACK * HEAD_DIM ** -0.5
    upper = qn[:, :, None] * kn[:, None, :] * scale + c_first[:, :, None] - c_last[:, None, :]
    lower = -(qn * kn * scale)[:, :, None]
    diag = jnp.eye(nb, dtype=bool)[None]
    return diag | (upper - lower >= -FOX_UNDERFLOW)


def _fox_kernel(qmap_ref, kmap_ref, need_ref, keff_ref, q_ref, k_ref, v_ref, cq_ref, ck_ref, o_ref,
                m_ref, l_ref, acc_ref):
    bh, step = pl.program_id(0), pl.program_id(1)
    qi, ki = qmap_ref[step], kmap_ref[step]

    @pl.when(ki == 0)
    def _():
        m_ref[...] = jnp.full_like(m_ref, NEG_BIG)
        l_ref[...] = jnp.zeros_like(l_ref)
        acc_ref[...] = jnp.zeros_like(acc_ref)

    def update(diagonal):
        q = (q_ref[...].astype(F32) * (HEAD_DIM ** -0.5)).astype(BF16)
        s = lax.dot_general(q, k_ref[...], NT_DIMS, preferred_element_type=F32)
        s = s + (cq_ref[0, :, 0:1] - ck_ref[0])
        if diagonal:
            row = lax.broadcasted_iota(jnp.int32, s.shape, 0)
            col = lax.broadcasted_iota(jnp.int32, s.shape, 1)
            s = jnp.where(row >= col, s, NEG_BIG)
        m_prev = m_ref[...]
        m_new = jnp.maximum(m_prev, jnp.max(s, axis=-1, keepdims=True))
        alpha = jnp.exp(m_prev - m_new)
        p = jnp.exp(s - m_new)
        l_ref[...] = alpha * l_ref[...] + jnp.sum(p, axis=-1, keepdims=True)
        acc_ref[...] = alpha * acc_ref[...] + jnp.dot(p.astype(BF16), v_ref[...], preferred_element_type=F32)
        m_ref[...] = m_new

    @pl.when(jnp.logical_and(ki < qi, need_ref[bh, step] == 1))
    def _():
        update(False)

    @pl.when(ki == qi)
    def _():
        update(True)
        o_ref[...] = (acc_ref[...] / l_ref[...]).astype(o_ref.dtype)


def _fox(h, c, batch, seq, blk=1024):
    t = h.shape[0]
    nb = seq // blk
    c3 = c[:, None, :]
    col0 = COL_FQ * BRANCH_WIDTH // HEAD_DIM
    pairs = [(qi, ki) for qi in range(nb) for ki in range(qi + 1)]
    qmap = jnp.asarray([p[0] for p in pairs], jnp.int32)
    kmap = jnp.asarray([p[1] for p in pairs], jnp.int32)
    needed = _fox_needed_blocks(h, c, batch, seq, blk)
    need_steps, key_steps = [], []
    for qi in range(nb):
        seg = needed[:, qi, :qi + 1]
        need_steps.append(seg)
        key_steps.append(lax.cummin(jnp.where(seg, jnp.arange(qi + 1, dtype=jnp.int32), qi), axis=1, reverse=True))
    need = jnp.concatenate(need_steps, axis=1).astype(jnp.int32)
    keff = jnp.concatenate(key_steps, axis=1).astype(jnp.int32)

    def qspec(off):
        return pl.BlockSpec((blk, HEAD_DIM),
                            lambda bh, s, qm, km, nd, ke: ((bh // N_HEADS) * nb + qm[s], col0 + off + bh % N_HEADS))

    def kspec(off):
        return pl.BlockSpec((blk, HEAD_DIM),
                            lambda bh, s, qm, km, nd, ke: ((bh // N_HEADS) * nb + ke[bh, s],
                                                           col0 + off + bh % N_HEADS))

    grid_spec = pltpu.PrefetchScalarGridSpec(
        num_scalar_prefetch=4,
        grid=(batch * N_HEADS, len(pairs)),
        in_specs=[
            qspec(0), kspec(N_HEADS), kspec(2 * N_HEADS),
            pl.BlockSpec((1, 1, blk), lambda bh, s, qm, km, nd, ke: (bh, 0, qm[s])),
            pl.BlockSpec((1, 1, blk), lambda bh, s, qm, km, nd, ke: (bh, 0, ke[bh, s])),
        ],
        out_specs=pl.BlockSpec((blk, HEAD_DIM),
                               lambda bh, s, qm, km, nd, ke: ((bh // N_HEADS) * nb + qm[s], bh % N_HEADS)),
        scratch_shapes=[pltpu.VMEM((blk, 1), F32), pltpu.VMEM((blk, 1), F32), pltpu.VMEM((blk, HEAD_DIM), F32)],
    )
    return pl.pallas_call(
        _fox_kernel,
        grid_spec=grid_spec,
        out_shape=jax.ShapeDtypeStruct((t, BRANCH_WIDTH), BF16),
        compiler_params=_params(2),
        name="fox_attention",
    )(qmap, kmap, need, keff, h, h, h, c3, c3)


def _merge_kernel(x_ref, a_ref, b_ref, c_ref, d_ref, wg_ref, bg_ref, wb_ref, wo_ref, g_ref, beta_ref,
                  o_ref, ob_ref, ot_ref, *, alpha):
    x = x_ref[...]
    xb = x.astype(BF16)
    merged = jnp.zeros(x.shape, F32)
    for m, br in enumerate((a_ref, b_ref, c_ref, d_ref)):
        sl = slice(m * D_MODEL, (m + 1) * D_MODEL)
        gate = _sigmoid(jnp.dot(xb, wg_ref[:, sl], preferred_element_type=F32) + bg_ref[:, sl])
        merged = merged + jnp.dot(br[...], wb_ref[m], preferred_element_type=F32) * gate
    mix = jnp.dot(merged.astype(BF16), wo_ref[...], preferred_element_type=F32)
    y = _layer_norm(alpha * x + mix, g_ref[...], beta_ref[...])
    o_ref[...] = y
    ob_ref[...] = y.astype(BF16)
    ot_ref[...] = y.T.astype(BF16)


def _merge(x, branches, w_gate, b_gate, w_branch, w_out, ln_g, ln_b, alpha, tm=256):
    t = x.shape[0]
    row = lambda width: pl.BlockSpec((tm, width), lambda i: (i, 0))
    const = lambda *shape: pl.BlockSpec(shape, lambda i: (0,) * len(shape))
    return pl.pallas_call(
        functools.partial(_merge_kernel, alpha=alpha),
        grid=(t // tm,),
        in_specs=[row(D_MODEL)] + [row(BRANCH_WIDTH)] * 4 + [
            const(D_MODEL, 4 * D_MODEL), const(1, 4 * D_MODEL), const(4, BRANCH_WIDTH, D_MODEL),
            const(D_MODEL, D_MODEL), const(1, D_MODEL), const(1, D_MODEL)],
        out_specs=[row(D_MODEL), row(D_MODEL), pl.BlockSpec((D_MODEL, tm), lambda i: (0, i))],
        out_shape=[jax.ShapeDtypeStruct((t, D_MODEL), F32), jax.ShapeDtypeStruct((t, D_MODEL), BF16),
                   jax.ShapeDtypeStruct((D_MODEL, t), BF16)],
        compiler_params=_params(1),
        name="merge_out_ln",
    )(x, *branches, w_gate.astype(BF16), b_gate[None, :], w_branch.astype(BF16), w_out.astype(BF16),
      ln_g[None, :], ln_b[None, :])


def _sorting_network(n):
    pairs, p = [], 1
    while p < n:
        k = p
        while k >= 1:
            for j in range(k % p, n - k, 2 * k):
                for i in range(min(k, n - j - k)):
                    if (i + j) // (2 * p) == (i + j + k) // (2 * p):
                        pairs.append((i + j, i + j + k))
            k //= 2
        p *= 2
    return pairs


def _sorted_stack(s):
    v = [s[k * SUBLANES:(k + 1) * SUBLANES, :] for k in range(s.shape[0] // SUBLANES)]
    for i, j in _sorting_network(len(v)):
        v[i], v[j] = jnp.maximum(v[i], v[j]), jnp.minimum(v[i], v[j])
    return v


def _pop_largest(stacks, count, floor, with_hits=False):
    stacks = [list(g) for g in stacks]
    tops, hits = [], []
    for it in range(count):
        m = functools.reduce(jnp.maximum, [jnp.max(g[0], axis=0, keepdims=True) for g in stacks])
        tops.append(m)
        left = count - it - 1
        n_hit = 0.0
        for g in stacks:
            hit = g[0] >= m
            if with_hits:
                n_hit = n_hit + jnp.sum(jnp.where(hit, 1.0, 0.0), axis=0, keepdims=True)
            g[:] = [jnp.where(hit, g[k + 1] if k + 1 < len(g) else floor, g[k]) for k in range(min(len(g), left))]
        hits.append(n_hit)
    return (tops, hits) if with_hits else tops


def _candidate_stacks(e1, e2):
    sub = lax.broadcasted_iota(jnp.int32, (SUBLANES, e1.shape[1]), 0)
    prods = [(e1 * e2[b:b + 1, :]).astype(F32) for b in range(PEER_TOPK)]
    low = [jnp.where(sub < PEER_TOPK // (b + 1), prods[b][:SUBLANES, :], -1.0) for b in range(PEER_TOPK)]
    return [low, [prods[0][SUBLANES:, :]]]


def _select_kernel(x_ref, wq_ref, k_ref, e1_ref, e2_ref, pt_ref):
    q = jnp.dot(x_ref[...], wq_ref[...], preferred_element_type=F32).astype(BF16)
    for h in range(PEER_HEADS):
        dense, top_e = [], []
        for half in range(2):
            c0 = (2 * h + half) * PEER_N_KEYS
            s = lax.dot_general(k_ref[half, h], q[:, c0:c0 + PEER_N_KEYS], NT_DIMS,
                                preferred_element_type=F32)
            tops = jnp.concatenate(_pop_largest([_sorted_stack(s)], PEER_TOPK, -jnp.inf), axis=0)
            dense.append(jnp.where(s >= tops[PEER_TOPK - 1:, :], jnp.exp(s - tops[0:1, :]), 0.0))
            top_e.append(jnp.exp(tops - tops[0:1, :]))
        z = functools.reduce(jnp.add, _pop_largest(_candidate_stacks(top_e[0], top_e[1]), PEER_TOPK, -1.0))
        inv_z = 0.5 / z
        e1n = (top_e[0] * inv_z).astype(BF16)
        e2b = top_e[1].astype(BF16)
        best, repeats = _pop_largest(_candidate_stacks(e1n, e2b), PEER_TOPK, -1.0, with_hits=True)
        count = jnp.zeros_like(inv_z)
        thr = jnp.zeros_like(inv_z)
        for m, c in zip(best, repeats):
            thr = jnp.where(count < PEER_TOPK, m, thr)
            count = count + c
        e1_ref[:, h, :] = (dense[0] * inv_z).astype(BF16).astype(F32)
        e2_ref[h] = dense[1].astype(BF16)
        pt_ref[h:h + 1, :] = thr


def _peer_select(xb, wq, keys, tm=256):
    t = xb.shape[0]
    nq = wq.shape[1]
    const = lambda *shape: pl.BlockSpec(shape, lambda i: (0,) * len(shape))
    return pl.pallas_call(
        _select_kernel,
        grid=(t // tm,),
        in_specs=[pl.BlockSpec((tm, D_MODEL), lambda i: (i, 0)), const(D_MODEL, nq),
                  const(2, PEER_HEADS, PEER_N_KEYS, PEER_N_KEYS)],
        out_specs=[pl.BlockSpec((PEER_N_KEYS, PEER_HEADS, tm), lambda i: (0, 0, i)),
                   pl.BlockSpec((PEER_HEADS, PEER_N_KEYS, tm), lambda i: (0, 0, i)),
                   pl.BlockSpec((PEER_HEADS, tm), lambda i: (0, i))],
        out_shape=[jax.ShapeDtypeStruct((PEER_N_KEYS, PEER_HEADS, t), F32),
                   jax.ShapeDtypeStruct((PEER_HEADS, PEER_N_KEYS, t), BF16),
                   jax.ShapeDtypeStruct((PEER_HEADS, t), F32)],
        compiler_params=_params(1),
        name="peer_select",
    )(xb, wq.astype(BF16), keys.astype(BF16))


def _peer_dense_kernel(xt_ref, u_ref, v_ref, e1_ref, e2_ref, pt_ref, x_ref, g_ref, beta_ref, o_ref, ob_ref,
                       acc_ref, ht0_ref, ht1_ref, wt0_ref, wt1_ref, *, alpha, tb, te, n_tiles, n_steps):
    s = pl.program_id(0)
    rows_per_step = te // PEER_N_KEYS
    c_tile = jnp.clip(s - 2, 0, n_steps - 1) % n_tiles

    @pl.when(s == 0)
    def _():
        ht1_ref[...] = jnp.zeros_like(ht1_ref)
        wt0_ref[...] = jnp.zeros_like(wt0_ref)

    @pl.when(c_tile == 0)
    def _():
        acc_ref[...] = jnp.zeros_like(acc_ref)

    def stages(ht_w, ht_r, wt_w, wt_r):
        tile = jnp.clip(s - 1, 0, n_steps - 1) % n_tiles

        def stage_a(k):
            ks = slice(k * (te // PEER_PIECES), (k + 1) * (te // PEER_PIECES))
            ht_w[ks, :] = jnp.dot(u_ref[ks, :], xt_ref[...], preferred_element_type=F32).astype(BF16)

        def stage_c(k):
            ks = slice(k * (tb // PEER_PIECES), (k + 1) * (tb // PEER_PIECES))
            acc_ref[ks, :] += lax.dot_general(wt_r[:, ks], v_ref[...], TN_DIMS, preferred_element_type=F32)

        def stage_b(rl, lt):
            r = tile * rows_per_step + rl
            ls = slice(lt * LANES, (lt + 1) * LANES)
            e1_rows = e1_ref[r, :, ls].astype(BF16)
            thr_rows = pt_ref[:, ls].astype(BF16)
            e1_b = [jnp.broadcast_to(e1_rows[h:h + 1, :], (PEER_SUB, LANES)) for h in range(PEER_HEADS)]
            thr_b = [jnp.broadcast_to(thr_rows[h:h + 1, :], (PEER_SUB, LANES)) for h in range(PEER_HEADS)]
            for sub in range(PEER_N_KEYS // PEER_SUB):
                js = slice(sub * PEER_SUB, (sub + 1) * PEER_SUB)
                rs = slice(rl * PEER_N_KEYS + sub * PEER_SUB, rl * PEER_N_KEYS + (sub + 1) * PEER_SUB)
                gate = None
                for h in range(PEER_HEADS):
                    p = e1_b[h] * e2_ref[h, js, ls]
                    sel = jnp.where(p >= thr_b[h], p, jnp.zeros_like(p))
                    gate = sel if gate is None else gate + sel
                wt_w[rs, ls] = _gelu2(ht_r[rs, ls]) * gate

        units = [(rl, lt) for rl in range(rows_per_step) for lt in range(tb // LANES)]
        per_slot = len(units) // (2 * PEER_PIECES)
        for k in range(PEER_PIECES):
            stage_a(k)
            for rl, lt in units[2 * k * per_slot:(2 * k + 1) * per_slot]:
                stage_b(rl, lt)
            stage_c(k)
            for rl, lt in units[(2 * k + 1) * per_slot:(2 * k + 2) * per_slot]:
                stage_b(rl, lt)

    @pl.when(s % 2 == 0)
    def _():
        stages(ht0_ref, ht1_ref, wt1_ref, wt0_ref)

    @pl.when(s % 2 == 1)
    def _():
        stages(ht1_ref, ht0_ref, wt0_ref, wt1_ref)

    @pl.when(jnp.logical_and(s >= 2, c_tile == n_tiles - 1))
    def _():
        y = _layer_norm(alpha * x_ref[...] + acc_ref[...], g_ref[...], beta_ref[...])
        o_ref[...] = y
        ob_ref[...] = y.astype(BF16)


def _peer_dense(x, xt, u_tab, v_tab, layer, e1, e2, pt, ln_g, ln_b, alpha, tb=1024, te=512):
    t = x.shape[0]
    n_tiles = u_tab.shape[1] // te
    n_steps = (t // tb) * n_tiles
    assert n_tiles % 2 == 0
    step_a = lambda s: jnp.minimum(s, n_steps - 1)
    step_b = lambda s: jnp.clip(s - 1, 0, n_steps - 1)
    step_c = lambda s: jnp.clip(s - 2, 0, n_steps - 1)
    const = lambda *shape: pl.BlockSpec(shape, lambda s: (0,) * len(shape))
    row_c = pl.BlockSpec((tb, D_MODEL), lambda s: (step_c(s) // n_tiles, 0))
    work = pltpu.VMEM((te, tb), BF16)
    return pl.pallas_call(
        functools.partial(_peer_dense_kernel, alpha=alpha, tb=tb, te=te, n_tiles=n_tiles, n_steps=n_steps),
        grid=(n_steps + 2,),
        in_specs=[pl.BlockSpec((D_MODEL, tb), lambda s: (0, step_a(s) // n_tiles)),
                  pl.BlockSpec((None, te, D_MODEL), lambda s: (layer, step_a(s) % n_tiles, 0)),
                  pl.BlockSpec((None, te, D_MODEL), lambda s: (layer, step_c(s) % n_tiles, 0)),
                  pl.BlockSpec((PEER_N_KEYS, PEER_HEADS, tb), lambda s: (0, 0, step_b(s) // n_tiles)),
                  pl.BlockSpec((PEER_HEADS, PEER_N_KEYS, tb), lambda s: (0, 0, step_b(s) // n_tiles)),
                  pl.BlockSpec((PEER_HEADS, tb), lambda s: (0, step_b(s) // n_tiles)),
                  row_c, const(1, D_MODEL), const(1, D_MODEL)],
        out_specs=[row_c, row_c],
        out_shape=[jax.ShapeDtypeStruct((t, D_MODEL), F32), jax.ShapeDtypeStruct((t, D_MODEL), BF16)],
        scratch_shapes=[pltpu.VMEM((tb, D_MODEL), F32), work, work, work, work],
        compiler_params=_params(1, vmem=PEER_DENSE_VMEM),
        name="peer_dense",
    )(xt, u_tab, v_tab, e1, e2, pt, x, ln_g[None, :], ln_b[None, :])


def kernel(x, w_in, b_fox_f, pool_w, pool_scale, hgrn_lb_logits, hgrn_norm_g, gmlp_ln_g, gmlp_ln_b, gmlp_ws, gmlp_bs,
           w_gate, b_gate, w_branch, w_out, ln1_g, ln1_b, peer_wq, peer_keys, peer_u, peer_v, ln2_g, ln2_b):
    batch, seq, d = x.shape
    depth = w_in.shape[0]
    alpha = (2.0 * depth) ** 0.25
    n_main = 10 * BRANCH_WIDTH

    p_lb = jax.nn.softmax(hgrn_lb_logits.astype(F32), axis=0)
    lower_bounds = jnp.cumsum(p_lb, axis=0) - p_lb[0]

    xf = x.reshape(batch * seq, d)
    xb = xf.astype(BF16)
    u_all, v_all = peer_u.astype(BF16), peer_v.astype(BF16)
    for l in range(depth):
        w_ff = jnp.pad(lax.slice(w_in, (l, 0, n_main), (l + 1, d, w_in.shape[2]))[0], ((0, 0), (0, LANES - N_HEADS)))
        h = _matmul(xb, w_in, BF16, tm=2048, tn=1024, n=n_main, layer=l)
        ffp = _matmul(xb, w_ff, F32, tm=2048, tn=LANES)

        a, c = _local_mixers(h, seq, pool_w[l], pool_scale[l], gmlp_ln_g[l], gmlp_ln_b[l], gmlp_ws[l], gmlp_bs[l])
        b = _hgrn(h, batch, seq, lower_bounds[l], hgrn_norm_g[l])
        cdec = _fox_decay(ffp, b_fox_f[l], batch, seq)
        dd = _fox(h, cdec, batch, seq)

        x1, x1b, x1t = _merge(xf, (a, b, c, dd), w_gate[l], b_gate[l], w_branch[l], w_out[l], ln1_g[l], ln1_b[l],
                              alpha)

        e1, e2, pt = _peer_select(x1b, peer_wq[l], peer_keys[l])
        xf, xb = _peer_dense(x1, x1t, u_all, v_all, l, e1, e2, pt, ln2_g[l], ln2_b[l], alpha)
    return xf.reshape(batch, seq, d)
```

```python
import functools

import numpy as np
import jax
import jax.numpy as jnp
from jax import lax
from jax.experimental import pallas as pl
from jax.experimental.pallas import tpu as pltpu

F32 = jnp.float32
BF16 = jnp.bfloat16

D_MODEL = 1024
BRANCH_WIDTH = 512
HEAD_DIM = 128
N_HEADS = BRANCH_WIDTH // HEAD_DIM
POOL_WINDOWS = (2, 4, 8, 16)
POOL_HALO = 16
GMLP_CHUNK = 128
HGRN_CHUNK = 128
HGRN_SUB = 16
PEER_HEADS = 8
PEER_N_KEYS = 128
PEER_TOPK = 16
PEER_SUB = 32
PEER_PIECES = 2
COL_POOL, COL_HQ, COL_HF, COL_HI, COL_HG, COL_GU, COL_GV, COL_FQ, COL_FK, COL_FV = range(10)
LN_EPS = 1e-5
RMS_EPS = 1e-6
NEG_BIG = -1e30
FOX_UNDERFLOW = 110.0
FOX_BOUND_SLACK = 1.02

LANES = 128
SUBLANES = 8
VMEM_LIMIT = 48 * 1024 * 1024
PEER_DENSE_VMEM = 58 * 1024 * 1024

NT_DIMS = (((1,), (1,)), ((), ()))
TN_DIMS = (((0,), (0,)), ((), ()))


def _params(n_grid, vmem=VMEM_LIMIT, flags=None):
    return pltpu.CompilerParams(dimension_semantics=("arbitrary",) * n_grid, vmem_limit_bytes=vmem, flags=flags)


def _gelu(x):
    return 0.5 * x * (1.0 + jnp.tanh(0.7978845608028654 * (x + 0.044715 * (x * x * x))))


def _gelu2(x):
    return x * (1.0 + jnp.tanh(x * (0.7978845608028654 + (0.7978845608028654 * 0.044715) * (x * x))))


def _sigmoid(x):
    return 1.0 / (1.0 + jnp.exp(-x))


def _split3(x):
    h1 = x.astype(BF16)
    r1 = x - h1.astype(F32)
    h2 = r1.astype(BF16)
    h3 = (r1 - h2.astype(F32)).astype(BF16)
    return h1, h2, h3


def _dot3(m, x):
    h1, h2, h3 = _split3(x)
    return (jnp.dot(m, h1, preferred_element_type=F32) + jnp.dot(m, h2, preferred_element_type=F32)
            + jnp.dot(m, h3, preferred_element_type=F32))


def _layer_norm(y, g, b):
    mu = jnp.mean(y, axis=-1, keepdims=True)
    yc = y - mu
    var = jnp.mean(yc * yc, axis=-1, keepdims=True)
    return yc * lax.rsqrt(var + LN_EPS) * g + b


def _mm_kernel(x_ref, w_ref, o_ref):
    o_ref[...] = jnp.dot(x_ref[...], w_ref[...].astype(BF16), preferred_element_type=F32).astype(o_ref.dtype)


def _matmul(x, w, out_dtype, tm, tn, n=None, layer=None):
    m, k = x.shape
    n = w.shape[-1] if n is None else n
    if layer is None:
        w_spec = pl.BlockSpec((k, tn), lambda i, j: (0, j))
    else:
        w_spec = pl.BlockSpec((None, k, tn), lambda i, j: (layer, 0, j))
    return pl.pallas_call(
        _mm_kernel,
        grid=(m // tm, n // tn),
        in_specs=[pl.BlockSpec((tm, k), lambda i, j: (i, 0)), w_spec],
        out_specs=pl.BlockSpec((tm, tn), lambda i, j: (i, j)),
        out_shape=jax.ShapeDtypeStruct((m, n), out_dtype),
        compiler_params=_params(2),
        name="proj_matmul",
    )(x, w)


def _local_kernel(p_ref, halo_ref, gu_ref, gv_ref, sm_ref, sh_ref, pw_ref, ps_ref, lg_ref, lb_ref, ws_ref,
                  bsb_ref, a_ref, c_ref, *, rows, tiles_per_seq):
    t_in_seq = pl.program_id(0) % tiles_per_seq
    p = p_ref[...]
    halo = halo_ref[...]
    halo = jnp.where(t_in_seq == 0, jnp.zeros_like(halo), halo)
    pos = t_in_seq * rows + lax.broadcasted_iota(jnp.int32, (rows, HEAD_DIM), 0)
    for g, w in enumerate(POOL_WINDOWS):
        sl = slice(g * HEAD_DIM, (g + 1) * HEAD_DIM)
        sums = (jnp.dot(sm_ref[g], p[:, sl], preferred_element_type=F32)
                + jnp.dot(sh_ref[g], halo[:, sl], preferred_element_type=F32))
        cnt = jnp.minimum(pos + 1, w).astype(F32)
        d = sums / cnt - p[:, sl].astype(F32)
        y = jnp.dot(d.astype(BF16), pw_ref[g], preferred_element_type=F32) * ps_ref[:, sl]
        a_ref[:, sl] = y.astype(a_ref.dtype)

    u = _gelu(gu_ref[...].astype(F32))
    v = _layer_norm(_gelu(gv_ref[...].astype(F32)), lg_ref[...], lb_ref[...]).astype(BF16)
    r = lax.broadcasted_iota(jnp.int32, (GMLP_CHUNK, GMLP_CHUNK), 0)
    c = lax.broadcasted_iota(jnp.int32, (GMLP_CHUNK, GMLP_CHUNK), 1)
    for g in range(N_HEADS):
        sl = slice(g * HEAD_DIM, (g + 1) * HEAD_DIM)
        wsc = jnp.where(r >= c, ws_ref[g], 0.0).astype(BF16)
        for ch in range(rows // GMLP_CHUNK):
            rs = slice(ch * GMLP_CHUNK, (ch + 1) * GMLP_CHUNK)
            s = jnp.dot(wsc, v[rs, sl], preferred_element_type=F32) + bsb_ref[:, sl]
            c_ref[rs, sl] = (u[rs, sl] * s).astype(c_ref.dtype)


def _pool_sum_matrices(rows):
    t = np.arange(rows)[:, None]
    s = np.arange(rows)[None, :]
    u = np.arange(POOL_HALO)[None, :]
    main = np.stack([((s <= t) & (s > t - w)) for w in POOL_WINDOWS]).astype(np.float32)
    halo = np.stack([(u >= t - w + POOL_HALO + 1) for w in POOL_WINDOWS]).astype(np.float32)
    return jnp.asarray(main, BF16), jnp.asarray(halo, BF16)


def _local_mixers(h, seq, pool_w, pool_scale, ln_g, ln_b, ws, bs, rows=256):
    t = h.shape[0]
    tiles_per_seq = seq // rows
    sm, sh = _pool_sum_matrices(rows)
    bsb = jnp.repeat(bs.T, HEAD_DIM, axis=1)
    halo_blocks = rows // POOL_HALO
    const = lambda *shape: pl.BlockSpec(shape, lambda i: (0,) * len(shape))
    return pl.pallas_call(
        functools.partial(_local_kernel, rows=rows, tiles_per_seq=tiles_per_seq),
        grid=(t // rows,),
        in_specs=[
            pl.BlockSpec((rows, BRANCH_WIDTH), lambda i: (i, 0)),
            pl.BlockSpec((POOL_HALO, BRANCH_WIDTH), lambda i: (jnp.maximum(i * halo_blocks - 1, 0), 0)),
            pl.BlockSpec((rows, BRANCH_WIDTH), lambda i: (i, COL_GU)),
            pl.BlockSpec((rows, BRANCH_WIDTH), lambda i: (i, COL_GV)),
            const(4, rows, rows), const(4, rows, POOL_HALO), const(4, HEAD_DIM, HEAD_DIM),
            const(1, BRANCH_WIDTH), const(1, BRANCH_WIDTH), const(1, BRANCH_WIDTH),
            const(4, GMLP_CHUNK, GMLP_CHUNK), const(GMLP_CHUNK, BRANCH_WIDTH),
        ],
        out_specs=[pl.BlockSpec((rows, BRANCH_WIDTH), lambda i: (i, 0))] * 2,
        out_shape=[jax.ShapeDtypeStruct((t, BRANCH_WIDTH), BF16)] * 2,
        compiler_params=_params(1),
        name="pool_gmlp",
    )(h, h, h, h, sm, sh, pool_w.astype(BF16), pool_scale[None, :], ln_g[None, :], ln_b[None, :], ws, bsb)


def _hgrn_kernel(q_ref, f_ref, i_ref, g_ref, lb_ref, ng_ref, tri_ref, o_ref, st_ref):
    C, S = HGRN_CHUNK, HGRN_SUB

    @pl.when(pl.program_id(1) == 0)
    def _():
        st_ref[...] = jnp.zeros_like(st_ref)

    zq = q_ref[...].astype(F32)
    qs = zq * _sigmoid(zq)
    kk = (1.0 - lb_ref[...]) * _sigmoid(-f_ref[...].astype(F32))
    logf = jnp.log(1.0 - kk)
    vv = i_ref[...].astype(F32)
    vb = vv.astype(BF16)
    bcs = _dot3(tri_ref[...], logf)
    b_last = bcs[C - 1:C, :]
    qe = (qs * jnp.exp(bcs)).astype(BF16)
    kd = (kk * jnp.exp(b_last - bcs)).astype(BF16)
    decay = jnp.exp(b_last)

    row = lax.broadcasted_iota(jnp.int32, (C, HEAD_DIM), 0)
    row_in_sub = row % S

    o = jnp.zeros((C, BRANCH_WIDTH), F32)
    d_acc = jnp.zeros((C, BRANCH_WIDTH), F32)
    for delta in range(S):
        if delta == 0:
            x = qs * kk
            v_s = vv
        else:
            d_acc = d_acc + pltpu.roll(logf, delta - 1, 0) if delta > 1 else d_acc + logf
            x = qs * pltpu.roll(kk, delta, 0) * jnp.exp(d_acc)
            v_s = pltpu.roll(vv, delta, 0)
        parts = []
        for h in range(N_HEADS):
            sl = slice(h * HEAD_DIM, (h + 1) * HEAD_DIM)
            w = jnp.sum(x[:, sl], axis=-1, keepdims=True)
            parts.append(jnp.where(row_in_sub >= delta, w, 0.0) * v_s[:, sl])
        o = o + jnp.concatenate(parts, axis=1)

    heads = [slice(h * HEAD_DIM, (h + 1) * HEAD_DIM) for h in range(N_HEADS)]
    states = [st_ref[h] for h in range(N_HEADS)]
    inter = [lax.dot_general(qe[:, sl], st.astype(BF16), NT_DIMS, preferred_element_type=F32)
             for sl, st in zip(heads, states)]
    weights = [[jnp.zeros((S, C), F32)] for _ in heads]
    for sb in range(1, C // S):
        r0 = sb * S
        for h, sl in enumerate(heads):
            ref = bcs[r0 - 1:r0, sl]
            q_i = (qs[r0:r0 + S, sl] * jnp.exp(bcs[r0:r0 + S, sl] - ref)).astype(BF16)
            k_i = (kk[:, sl] * jnp.exp(jnp.where(row < r0, ref - bcs[:, sl], NEG_BIG))).astype(BF16)
            weights[h].append(lax.dot_general(q_i, k_i, NT_DIMS, preferred_element_type=F32))
    outs = []
    for h, sl in enumerate(heads):
        a_h = jnp.concatenate(weights[h], axis=0).astype(BF16)
        o_h = o[:, sl] + inter[h] + jnp.dot(a_h, vb[:, sl], preferred_element_type=F32)
        st_ref[h] = states[h] * decay[:, sl] + jnp.dot(vv[:, sl].T.astype(BF16), kd[:, sl],
                                                       preferred_element_type=F32)
        ms = jnp.mean(o_h * o_h, axis=-1, keepdims=True)
        outs.append(o_h * lax.rsqrt(ms + RMS_EPS))
    zg = g_ref[...].astype(F32)
    o_ref[...] = (jnp.concatenate(outs, axis=1) * ng_ref[...] * (zg * _sigmoid(zg))).astype(o_ref.dtype)


def _hgrn(h, batch, seq, lower_bound, norm_g):
    t = h.shape[0]
    C = HGRN_CHUNK
    n = seq // C
    tri = jnp.asarray(np.tril(np.ones((C, C), np.float32)), BF16)
    blk = lambda col: pl.BlockSpec((C, BRANCH_WIDTH), lambda b, c: (b * n + c, col))
    const = lambda *shape: pl.BlockSpec(shape, lambda b, c: (0,) * len(shape))
    return pl.pallas_call(
        _hgrn_kernel,
        grid=(batch, n),
        in_specs=[blk(COL_HQ), blk(COL_HF), blk(COL_HI), blk(COL_HG), const(1, BRANCH_WIDTH), const(1, BRANCH_WIDTH), const(C, C)],
        out_specs=pl.BlockSpec((C, BRANCH_WIDTH), lambda b, c: (b * n + c, 0)),
        out_shape=jax.ShapeDtypeStruct((t, BRANCH_WIDTH), BF16),
        scratch_shapes=[pltpu.VMEM((N_HEADS, HEAD_DIM, HEAD_DIM), F32)],
        compiler_params=_params(2),
        name="hgrn2",
    )(h, h, h, h, lower_bound[None, :], jnp.tile(norm_g, N_HEADS)[None, :], tri)


def _fox_decay_kernel(f_ref, bias_ref, up_ref, low_ref, c_ref):
    z = f_ref[...] + bias_ref[...]
    ls = jnp.minimum(z, 0.0) - jnp.log(1.0 + jnp.exp(-jnp.abs(z)))
    within = _dot3_right(ls, up_ref[...])
    tot = jnp.broadcast_to(within[:, LANES - 1:LANES], within.shape)
    c_ref[...] = within + _dot3(low_ref[...], tot)


def _dot3_right(x, m):
    h1, h2, h3 = _split3(x)
    return (jnp.dot(h1, m, preferred_element_type=F32) + jnp.dot(h2, m, preferred_element_type=F32)
            + jnp.dot(h3, m, preferred_element_type=F32))


def _fox_decay(ffp, b_fox_f, batch, seq):
    nrow = seq // LANES
    f = ffp[:, :N_HEADS].reshape(batch, seq, N_HEADS).transpose(0, 2, 1).reshape(batch * N_HEADS * nrow, LANES)
    rows = f.shape[0]
    bias = jnp.broadcast_to(jnp.tile(jnp.repeat(b_fox_f, nrow), batch)[:, None], (rows, LANES))
    up = jnp.asarray(np.triu(np.ones((LANES, LANES), np.float32)), BF16)
    grp = np.arange(rows) // nrow
    low = (grp[:, None] == grp[None, :]) & (np.arange(rows)[None, :] < np.arange(rows)[:, None])
    low = jnp.asarray(low.astype(np.float32), BF16)
    full = lambda *shape: pl.BlockSpec(shape, lambda: (0,) * len(shape))
    c = pl.pallas_call(
        _fox_decay_kernel,
        in_specs=[full(rows, LANES), full(rows, LANES), full(LANES, LANES), full(rows, rows)],
        out_specs=full(rows, LANES),
        out_shape=jax.ShapeDtypeStruct((rows, LANES), F32),
        name="fox_decay",
    )(f, bias, up, low)
    return c.reshape(batch * N_HEADS, seq)


def _fox_norm_kernel(q_ref, k_ref, o_ref):
    lane = lax.broadcasted_iota(jnp.int32, (1, LANES), 1)
    row = jnp.zeros((1, LANES), F32)
    for idx, ref in enumerate((q_ref, k_ref)):
        z = ref[...].astype(F32)
        for hd in range(N_HEADS):
            zh = z[:, hd * HEAD_DIM:(hd + 1) * HEAD_DIM]
            n2 = jnp.max(jnp.sum(zh * zh, axis=-1, keepdims=True), axis=0, keepdims=True)
            row = jnp.where(lane == idx * N_HEADS + hd, n2, row)
    o_ref[0] = row


def _fox_needed_blocks(h, c, batch, seq, blk):
    nb = seq // blk
    norms = pl.pallas_call(
        _fox_norm_kernel,
        grid=(batch * nb,),
        in_specs=[pl.BlockSpec((blk, BRANCH_WIDTH), lambda i: (i, COL_FQ)),
                  pl.BlockSpec((blk, BRANCH_WIDTH), lambda i: (i, COL_FK))],
        out_specs=pl.BlockSpec((1, 1, LANES), lambda i: (i, 0, 0)),
        out_shape=jax.ShapeDtypeStruct((batch * nb, 1, LANES), F32),
        compiler_params=_params(1),
        name="fox_norms",
    )(h, h)
    nr = jnp.sqrt(norms[:, 0, :2 * N_HEADS]).reshape(batch, nb, 2, N_HEADS)
    qn = nr[:, :, 0, :].transpose(0, 2, 1).reshape(batch * N_HEADS, nb)
    kn = nr[:, :, 1, :].transpose(0, 2, 1).reshape(batch * N_HEADS, nb)
    c_first = c[:, ::blk]
    c_last = c[:, blk - 1::blk]
    scale = FOX_BOUND_SLACK * HEAD_DIM ** -0.5
    upper = qn[:, :, None] * kn[:, None, :] * scale + c_first[:, :, None] - c_last[:, None, :]
    lower = -(qn * kn * scale)[:, :, None]
    diag = jnp.eye(nb, dtype=bool)[None]
    return diag | (upper - lower >= -FOX_UNDERFLOW)


def _fox_kernel(qmap_ref, kmap_ref, need_ref, keff_ref, q_ref, k_ref, v_ref, cq_ref, ck_ref, o_ref,
                m_ref, l_ref, acc_ref):
    bh, step = pl.program_id(0), pl.program_id(1)
    qi, ki = qmap_ref[step], kmap_ref[step]

    @pl.when(ki == 0)
    def _():
        m_ref[...] = jnp.full_like(m_ref, NEG_BIG)
        l_ref[...] = jnp.zeros_like(l_ref)
        acc_ref[...] = jnp.zeros_like(acc_ref)

    def update(diagonal):
        q = (q_ref[...].astype(F32) * (HEAD_DIM ** -0.5)).astype(BF16)
        s = lax.dot_general(q, k_ref[...], NT_DIMS, preferred_element_type=F32)
        s = s + (cq_ref[0, :, 0:1] - ck_ref[0])
        if diagonal:
            row = lax.broadcasted_iota(jnp.int32, s.shape, 0)
            col = lax.broadcasted_iota(jnp.int32, s.shape, 1)
            s = jnp.where(row >= col, s, NEG_BIG)
        m_prev = m_ref[...]
        m_new = jnp.maximum(m_prev, jnp.max(s, axis=-1, keepdims=True))
        alpha = jnp.exp(m_prev - m_new)
        p = jnp.exp(s - m_new)
        l_ref[...] = alpha * l_ref[...] + jnp.sum(p, axis=-1, keepdims=True)
        acc_ref[...] = alpha * acc_ref[...] + jnp.dot(p.astype(BF16), v_ref[...], preferred_element_type=F32)
        m_ref[...] = m_new

    @pl.when(jnp.logical_and(ki < qi, need_ref[bh, step] == 1))
    def _():
        update(False)

    @pl.when(ki == qi)
    def _():
        update(True)
        o_ref[...] = (acc_ref[...] / l_ref[...]).astype(o_ref.dtype)


def _fox(h, c, batch, seq, blk=1024):
    t = h.shape[0]
    nb = seq // blk
    c3 = c[:, None, :]
    col0 = COL_FQ * BRANCH_WIDTH // HEAD_DIM
    pairs = [(qi, ki) for qi in range(nb) for ki in range(qi + 1)]
    qmap = jnp.asarray([p[0] for p in pairs], jnp.int32)
    kmap = jnp.asarray([p[1] for p in pairs], jnp.int32)
    needed = _fox_needed_blocks(h, c, batch, seq, blk)
    need_steps, key_steps = [], []
    for qi in range(nb):
        seg = needed[:, qi, :qi + 1]
        need_steps.append(seg)
        key_steps.append(lax.cummin(jnp.where(seg, jnp.arange(qi + 1, dtype=jnp.int32), qi), axis=1, reverse=True))
    need = jnp.concatenate(need_steps, axis=1).astype(jnp.int32)
    keff = jnp.concatenate(key_steps, axis=1).astype(jnp.int32)

    def qspec(off):
        return pl.BlockSpec((blk, HEAD_DIM),
                            lambda bh, s, qm, km, nd, ke: ((bh // N_HEADS) * nb + qm[s], col0 + off + bh % N_HEADS))

    def kspec(off):
        return pl.BlockSpec((blk, HEAD_DIM),
                            lambda bh, s, qm, km, nd, ke: ((bh // N_HEADS) * nb + ke[bh, s],
                                                           col0 + off + bh % N_HEADS))

    grid_spec = pltpu.PrefetchScalarGridSpec(
        num_scalar_prefetch=4,
        grid=(batch * N_HEADS, len(pairs)),
        in_specs=[
            qspec(0), kspec(N_HEADS), kspec(2 * N_HEADS),
            pl.BlockSpec((1, 1, blk), lambda bh, s, qm, km, nd, ke: (bh, 0, qm[s])),
            pl.BlockSpec((1, 1, blk), lambda bh, s, qm, km, nd, ke: (bh, 0, ke[bh, s])),
        ],
        out_specs=pl.BlockSpec((blk, HEAD_DIM),
                               lambda bh, s, qm, km, nd, ke: ((bh // N_HEADS) * nb + qm[s], bh % N_HEADS)),
        scratch_shapes=[pltpu.VMEM((blk, 1), F32), pltpu.VMEM((blk, 1), F32), pltpu.VMEM((blk, HEAD_DIM), F32)],
    )
    return pl.pallas_call(
        _fox_kernel,
        grid_spec=grid_spec,
        out_shape=jax.ShapeDtypeStruct((t, BRANCH_WIDTH), BF16),
        compiler_params=_params(2),
        name="fox_attention",
    )(qmap, kmap, need, keff, h, h, h, c3, c3)


def _merge_kernel(x_ref, a_ref, b_ref, c_ref, d_ref, wg_ref, bg_ref, wb_ref, wo_ref, g_ref, beta_ref,
                  o_ref, ob_ref, ot_ref, *, alpha):
    x = x_ref[...]
    xb = x.astype(BF16)
    merged = jnp.zeros(x.shape, F32)
    for m, br in enumerate((a_ref, b_ref, c_ref, d_ref)):
        sl = slice(m * D_MODEL, (m + 1) * D_MODEL)
        gate = _sigmoid(jnp.dot(xb, wg_ref[:, sl], preferred_element_type=F32) + bg_ref[:, sl])
        merged = merged + jnp.dot(br[...], wb_ref[m], preferred_element_type=F32) * gate
    mix = jnp.dot(merged.astype(BF16), wo_ref[...], preferred_element_type=F32)
    y = _layer_norm(alpha * x + mix, g_ref[...], beta_ref[...])
    o_ref[...] = y
    ob_ref[...] = y.astype(BF16)
    ot_ref[...] = y.T.astype(BF16)


def _merge(x, branches, w_gate, b_gate, w_branch, w_out, ln_g, ln_b, alpha, tm=256):
    t = x.shape[0]
    row = lambda width: pl.BlockSpec((tm, width), lambda i: (i, 0))
    const = lambda *shape: pl.BlockSpec(shape, lambda i: (0,) * len(shape))
    return pl.pallas_call(
        functools.partial(_merge_kernel, alpha=alpha),
        grid=(t // tm,),
        in_specs=[row(D_MODEL)] + [row(BRANCH_WIDTH)] * 4 + [
            const(D_MODEL, 4 * D_MODEL), const(1, 4 * D_MODEL), const(4, BRANCH_WIDTH, D_MODEL),
            const(D_MODEL, D_MODEL), const(1, D_MODEL), const(1, D_MODEL)],
        out_specs=[row(D_MODEL), row(D_MODEL), pl.BlockSpec((D_MODEL, tm), lambda i: (0, i))],
        out_shape=[jax.ShapeDtypeStruct((t, D_MODEL), F32), jax.ShapeDtypeStruct((t, D_MODEL), BF16),
                   jax.ShapeDtypeStruct((D_MODEL, t), BF16)],
        compiler_params=_params(1),
        name="merge_out_ln",
    )(x, *branches, w_gate.astype(BF16), b_gate[None, :], w_branch.astype(BF16), w_out.astype(BF16),
      ln_g[None, :], ln_b[None, :])


def _sorting_network(n):
    pairs, p = [], 1
    while p < n:
        k = p
        while k >= 1:
            for j in range(k % p, n - k, 2 * k):
                for i in range(min(k, n - j - k)):
                    if (i + j) // (2 * p) == (i + j + k) // (2 * p):
                        pairs.append((i + j, i + j + k))
            k //= 2
        p *= 2
    return pairs


def _sorted_stack(s):
    v = [s[k * SUBLANES:(k + 1) * SUBLANES, :] for k in range(s.shape[0] // SUBLANES)]
    for i, j in _sorting_network(len(v)):
        v[i], v[j] = jnp.maximum(v[i], v[j]), jnp.minimum(v[i], v[j])
    return v


def _pop_largest(stacks, count, floor, with_hits=False):
    stacks = [list(g) for g in stacks]
    tops, hits = [], []
    for it in range(count):
        m = functools.reduce(jnp.maximum, [jnp.max(g[0], axis=0, keepdims=True) for g in stacks])
        tops.append(m)
        left = count - it - 1
        n_hit = 0.0
        for g in stacks:
            hit = g[0] >= m
            if with_hits:
                n_hit = n_hit + jnp.sum(jnp.where(hit, 1.0, 0.0), axis=0, keepdims=True)
            g[:] = [jnp.where(hit, g[k + 1] if k + 1 < len(g) else floor, g[k]) for k in range(min(len(g), left))]
        hits.append(n_hit)
    return (tops, hits) if with_hits else tops


def _candidate_stacks(e1, e2):
    sub = lax.broadcasted_iota(jnp.int32, (SUBLANES, e1.shape[1]), 0)
    prods = [(e1 * e2[b:b + 1, :]).astype(F32) for b in range(PEER_TOPK)]
    low = [jnp.where(sub < PEER_TOPK // (b + 1), prods[b][:SUBLANES, :], -1.0) for b in range(PEER_TOPK)]
    return [low, [prods[0][SUBLANES:, :]]]


def _select_kernel(x_ref, wq_ref, k_ref, e1_ref, e2_ref, pt_ref):
    q = jnp.dot(x_ref[...], wq_ref[...], preferred_element_type=F32).astype(BF16)
    for h in range(PEER_HEADS):
        dense, top_e = [], []
        for half in range(2):
            c0 = (2 * h + half) * PEER_N_KEYS
            s = lax.dot_general(k_ref[half, h], q[:, c0:c0 + PEER_N_KEYS], NT_DIMS,
                                preferred_element_type=F32)
            tops = jnp.concatenate(_pop_largest([_sorted_stack(s)], PEER_TOPK, -jnp.inf), axis=0)
            dense.append(jnp.where(s >= tops[PEER_TOPK - 1:, :], jnp.exp(s - tops[0:1, :]), 0.0))
            top_e.append(jnp.exp(tops - tops[0:1, :]))
        z = functools.reduce(jnp.add, _pop_largest(_candidate_stacks(top_e[0], top_e[1]), PEER_TOPK, -1.0))
        inv_z = 0.5 / z
        e1n = (top_e[0] * inv_z).astype(BF16)
        e2b = top_e[1].astype(BF16)
        best, repeats = _pop_largest(_candidate_stacks(e1n.astype(F32), e2b.astype(F32)), PEER_TOPK, -1.0,
                                     with_hits=True)
        count = jnp.zeros_like(inv_z)
        thr = jnp.zeros_like(inv_z)
        for m, c in zip(best, repeats):
            thr = jnp.where(count < PEER_TOPK, m, thr)
            count = count + c
        e1_ref[:, h, :] = (dense[0] * inv_z).astype(BF16).astype(F32)
        e2_ref[h] = dense[1].astype(BF16)
        pt_ref[h:h + 1, :] = thr


def _peer_select(xb, wq, keys, tm=256):
    t = xb.shape[0]
    nq = wq.shape[1]
    const = lambda *shape: pl.BlockSpec(shape, lambda i: (0,) * len(shape))
    return pl.pallas_call(
        _select_kernel,
        grid=(t // tm,),
        in_specs=[pl.BlockSpec((tm, D_MODEL), lambda i: (i, 0)), const(D_MODEL, nq),
                  const(2, PEER_HEADS, PEER_N_KEYS, PEER_N_KEYS)],
        out_specs=[pl.BlockSpec((PEER_N_KEYS, PEER_HEADS, tm), lambda i: (0, 0, i)),
                   pl.BlockSpec((PEER_HEADS, PEER_N_KEYS, tm), lambda i: (0, 0, i)),
                   pl.BlockSpec((PEER_HEADS, tm), lambda i: (0, i))],
        out_shape=[jax.ShapeDtypeStruct((PEER_N_KEYS, PEER_HEADS, t), F32),
                   jax.ShapeDtypeStruct((PEER_HEADS, PEER_N_KEYS, t), BF16),
                   jax.ShapeDtypeStruct((PEER_HEADS, t), F32)],
        compiler_params=_params(1),
        name="peer_select",
    )(xb, wq.astype(BF16), keys.astype(BF16))


def _peer_dense_kernel(xt_ref, u_ref, v_ref, e1_ref, e2_ref, pt_ref, x_ref, g_ref, beta_ref, o_ref, ob_ref,
                       acc_ref, ht0_ref, ht1_ref, wt0_ref, wt1_ref, *, alpha, tb, te, n_tiles, n_steps):
    s = pl.program_id(0)
    rows_per_step = te // PEER_N_KEYS
    c_tile = jnp.clip(s - 2, 0, n_steps - 1) % n_tiles

    @pl.when(s == 0)
    def _():
        ht1_ref[...] = jnp.zeros_like(ht1_ref)
        wt0_ref[...] = jnp.zeros_like(wt0_ref)

    @pl.when(c_tile == 0)
    def _():
        acc_ref[...] = jnp.zeros_like(acc_ref)

    def stages(ht_w, ht_r, wt_w, wt_r):
        tile = jnp.clip(s - 1, 0, n_steps - 1) % n_tiles

        def stage_a(k):
            ks = slice(k * (te // PEER_PIECES), (k + 1) * (te // PEER_PIECES))
            ht_w[ks, :] = jnp.dot(u_ref[ks, :], xt_ref[...], preferred_element_type=F32).astype(BF16)

        def stage_c(k):
            ks = slice(k * (tb // PEER_PIECES), (k + 1) * (tb // PEER_PIECES))
            acc_ref[ks, :] += lax.dot_general(wt_r[:, ks], v_ref[...], TN_DIMS, preferred_element_type=F32)

        def stage_b(rl, lt):
            r = tile * rows_per_step + rl
            ls = slice(lt * LANES, (lt + 1) * LANES)
            e1_rows = e1_ref[r, :, ls]
            thr_rows = pt_ref[:, ls]
            e1_b = [jnp.broadcast_to(e1_rows[h:h + 1, :], (PEER_SUB, LANES)) for h in range(PEER_HEADS)]
            thr_b = [jnp.broadcast_to(thr_rows[h:h + 1, :], (PEER_SUB, LANES)) for h in range(PEER_HEADS)]
            for sub in range(PEER_N_KEYS // PEER_SUB):
                js = slice(sub * PEER_SUB, (sub + 1) * PEER_SUB)
                rs = slice(rl * PEER_N_KEYS + sub * PEER_SUB, rl * PEER_N_KEYS + (sub + 1) * PEER_SUB)
                gate = None
                for h in range(PEER_HEADS):
                    p = e1_b[h] * e2_ref[h, js, ls].astype(F32)
                    sel = jnp.where(p >= thr_b[h], p, jnp.zeros_like(p))
                    gate = sel if gate is None else gate + sel
                wt_w[rs, ls] = (_gelu2(ht_r[rs, ls].astype(F32)) * gate).astype(BF16)

        units = [(rl, lt) for rl in range(rows_per_step) for lt in range(tb // LANES)]
        per_slot = len(units) // (2 * PEER_PIECES)
        for k in range(PEER_PIECES):
            stage_a(k)
            for rl, lt in units[2 * k * per_slot:(2 * k + 1) * per_slot]:
                stage_b(rl, lt)
            stage_c(k)
            for rl, lt in units[(2 * k + 1) * per_slot:(2 * k + 2) * per_slot]:
                stage_b(rl, lt)

    @pl.when(s % 2 == 0)
    def _():
        stages(ht0_ref, ht1_ref, wt1_ref, wt0_ref)

    @pl.when(s % 2 == 1)
    def _():
        stages(ht1_ref, ht0_ref, wt0_ref, wt1_ref)

    @pl.when(jnp.logical_and(s >= 2, c_tile == n_tiles - 1))
    def _():
        y = _layer_norm(alpha * x_ref[...] + acc_ref[...], g_ref[...], beta_ref[...])
        o_ref[...] = y
        ob_ref[...] = y.astype(BF16)


def _peer_dense(x, xt, u_tab, v_tab, layer, e1, e2, pt, ln_g, ln_b, alpha, tb=1024, te=512):
    t = x.shape[0]
    n_tiles = u_tab.shape[1] // te
    n_steps = (t // tb) * n_tiles
    assert n_tiles % 2 == 0
    step_a = lambda s: jnp.minimum(s, n_steps - 1)
    step_b = lambda s: jnp.clip(s - 1, 0, n_steps - 1)
    step_c = lambda s: jnp.clip(s - 2, 0, n_steps - 1)
    const = lambda *shape: pl.BlockSpec(shape, lambda s: (0,) * len(shape))
    row_c = pl.BlockSpec((tb, D_MODEL), lambda s: (step_c(s) // n_tiles, 0))
    work = pltpu.VMEM((te, tb), BF16)
    return pl.pallas_call(
        functools.partial(_peer_dense_kernel, alpha=alpha, tb=tb, te=te, n_tiles=n_tiles, n_steps=n_steps),
        grid=(n_steps + 2,),
        in_specs=[pl.BlockSpec((D_MODEL, tb), lambda s: (0, step_a(s) // n_tiles)),
                  pl.BlockSpec((None, te, D_MODEL), lambda s: (layer, step_a(s) % n_tiles, 0)),
                  pl.BlockSpec((None, te, D_MODEL), lambda s: (layer, step_c(s) % n_tiles, 0)),
                  pl.BlockSpec((PEER_N_KEYS, PEER_HEADS, tb), lambda s: (0, 0, step_b(s) // n_tiles)),
                  pl.BlockSpec((PEER_HEADS, PEER_N_KEYS, tb), lambda s: (0, 0, step_b(s) // n_tiles)),
                  pl.BlockSpec((PEER_HEADS, tb), lambda s: (0, step_b(s) // n_tiles)),
                  row_c, const(1, D_MODEL), const(1, D_MODEL)],
        out_specs=[row_c, row_c],
        out_shape=[jax.ShapeDtypeStruct((t, D_MODEL), F32), jax.ShapeDtypeStruct((t, D_MODEL), BF16)],
        scratch_shapes=[pltpu.VMEM((tb, D_MODEL), F32), work, work, work, work],
        compiler_params=_params(1, vmem=PEER_DENSE_VMEM),
        name="peer_dense",
    )(xt, u_tab, v_tab, e1, e2, pt, x, ln_g[None, :], ln_b[None, :])


def kernel(x, w_in, b_fox_f, pool_w, pool_scale, hgrn_lb_logits, hgrn_norm_g, gmlp_ln_g, gmlp_ln_b, gmlp_ws, gmlp_bs,
           w_gate, b_gate, w_branch, w_out, ln1_g, ln1_b, peer_wq, peer_keys, peer_u, peer_v, ln2_g, ln2_b):
    batch, seq, d = x.shape
    depth = w_in.shape[0]
    alpha = (2.0 * depth) ** 0.25
    n_main = 10 * BRANCH_WIDTH

    p_lb = jax.nn.softmax(hgrn_lb_logits.astype(F32), axis=0)
    lower_bounds = jnp.cumsum(p_lb, axis=0) - p_lb[0]

    xf = x.reshape(batch * seq, d)
    xb = xf.astype(BF16)
    u_all, v_all = peer_u.astype(BF16), peer_v.astype(BF16)
    for l in range(depth):
        w_ff = jnp.pad(lax.slice(w_in, (l, 0, n_main), (l + 1, d, w_in.shape[2]))[0], ((0, 0), (0, LANES - N_HEADS)))
        h = _matmul(xb, w_in, BF16, tm=2048, tn=1024, n=n_main, layer=l)
        ffp = _matmul(xb, w_ff, F32, tm=2048, tn=LANES)

        a, c = _local_mixers(h, seq, pool_w[l], pool_scale[l], gmlp_ln_g[l], gmlp_ln_b[l], gmlp_ws[l], gmlp_bs[l])
        b = _hgrn(h, batch, seq, lower_bounds[l], hgrn_norm_g[l])
        cdec = _fox_decay(ffp, b_fox_f[l], batch, seq)
        dd = _fox(h, cdec, batch, seq)

        x1, x1b, x1t = _merge(xf, (a, b, c, dd), w_gate[l], b_gate[l], w_branch[l], w_out[l], ln1_g[l], ln1_b[l],
                              alpha)

        e1, e2, pt = _peer_select(x1b, peer_wq[l], peer_keys[l])
        xf, xb = _peer_dense(x1, x1t, u_all, v_all, l, e1, e2, pt, ln2_g[l], ln2_b[l], alpha)
    return xf.reshape(batch, seq, d)
```
